```python
import jax
import jax.numpy as jnp
from jax import lax
import numpy as np

D_MODEL = 1024
BATCH = 16
SEQ = 4096
DEPTH = 4

GRID_W = 64
CTX_LEN = 256
EPS = 1e-6
W_A = D_MODEL
H_A = 8
DH_A = W_A // H_A
CONV_W = 4
LRU_C = 8.0
W_B = D_MODEL // 2
POOL_WINDOWS = (2, 4, 8, 16)
G_B = len(POOL_WINDOWS)
DG_B = W_B // G_B
W_C = D_MODEL // 2
CHUNK = 128
G_C = 4
DG_C = W_C // G_C
N_BRANCH = 3
OFF_GA = W_A
OFF_B = 2 * W_A
OFF_U = OFF_B + W_B
OFF_V = OFF_U + W_C
OFF_G = OFF_V + W_C
IN_COLS = OFF_G + N_BRANCH * D_MODEL
N_EXPERTS = 32
TOP_K = 4
D_FF = D_MODEL
SWIGLU_LIMIT = 7.0
SWIGLU_ALPHA = 1.702
MOE_BLOCK = 512

kernel_name = 'hybrid_lru_pool_sgu_moe_prefix_trunk'


def _rmsnorm(x, g):
    xf = x.astype(jnp.float32)
    y = xf * lax.rsqrt(jnp.mean(xf * xf, axis=-1, keepdims=True) + EPS)
    return (y * g.astype(jnp.float32)).astype(x.dtype)


def _layernorm(x, g, b):
    xf = x.astype(jnp.float32)
    mu = jnp.mean(xf, axis=-1, keepdims=True)
    var = jnp.mean(jnp.square(xf - mu), axis=-1, keepdims=True)
    return ((xf - mu) * lax.rsqrt(var + EPS)).astype(x.dtype) * g + b


def _dwconv(x, w, b):
    y = lax.conv_general_dilated(
        x, w[:, None, :], window_strides=(1,),
        padding=[(CONV_W // 2, CONV_W - 1 - CONV_W // 2)],
        dimension_numbers=('NWC', 'WIO', 'NWC'),
        feature_group_count=x.shape[-1])
    return y + b


def _rglru_scan(xc, wa, ba, wx, bx, lam, h0, reverse, return_seq):
    xf = xc.astype(jnp.float32)
    bsz, t = xf.shape[0], xf.shape[1]
    xh = xf.reshape(bsz, t, H_A, DH_A)
    r = jax.nn.sigmoid(jnp.einsum('bthi,hij->bthj', xh, wa.astype(jnp.float32)).reshape(bsz, t, W_A) + ba)
    gi = jax.nn.sigmoid(jnp.einsum('bthi,hij->bthj', xh, wx.astype(jnp.float32)).reshape(bsz, t, W_A) + bx)
    log_a = -LRU_C * r * jax.nn.softplus(-lam.astype(jnp.float32))
    a = jnp.exp(log_a)
    inp = jnp.sqrt(-jnp.expm1(2.0 * log_a)) * gi * xf
    a_t = jnp.swapaxes(a, 0, 1)
    b_t = jnp.swapaxes(inp, 0, 1)
    if return_seq:
        def step(h, ab):
            h = ab[0] * h + ab[1]
            return h, h
        h_last, hs = lax.scan(step, h0, (a_t, b_t), reverse=reverse)
        return jnp.swapaxes(hs, 0, 1), h_last

    def step_state(h, ab):
        return ab[0] * h + ab[1], None
    h_last, _ = lax.scan(step_state, h0, (a_t, b_t), reverse=reverse)
    return None, h_last


def _lru_branch(xa_lat, xa_ctx, lp, ctx_seq):
    xcl = _dwconv(xa_lat, lp['conv_w'], lp['conv_b'])
    xcc = _dwconv(xa_ctx, lp['conv_w'], lp['conv_b'])
    h0 = jnp.zeros((xa_ctx.shape[0], W_A), jnp.float32)
    out_lat = None
    out_ctx = None
    for d in range(2):
        rev = d == 1
        args = (lp['lru_wa'][d], lp['lru_ba'][d], lp['lru_wx'][d], lp['lru_bx'][d], lp['lru_lambda'][d])
        hs_c, h_ctx_last = _rglru_scan(xcc, *args, h0, rev, ctx_seq)
        hs_l, _ = _rglru_scan(xcl, *args, h_ctx_last, rev, True)
        out_lat = hs_l if out_lat is None else out_lat + hs_l
        if ctx_seq:
            out_ctx = hs_c if out_ctx is None else out_ctx + hs_c
    out_lat = out_lat.astype(xa_lat.dtype)
    if ctx_seq:
        out_ctx = out_ctx.astype(xa_ctx.dtype)
    return out_lat, out_ctx


def _window_mean(x, k, axis):
    t = x.shape[axis]
    cs = jnp.cumsum(x.astype(jnp.float32), axis=axis)
    pad = [(0, 0)] * x.ndim
    pad[axis] = (1, 0)
    cs = jnp.pad(cs, pad)
    pos = jnp.arange(t)
    lo = jnp.clip(pos - k // 2, 0, t)
    hi = jnp.clip(pos + (k - k // 2), 0, t)
    s = jnp.take(cs, hi, axis=axis) - jnp.take(cs, lo, axis=axis)
    shape = [1] * x.ndim
    shape[axis] = t
    cnt = (hi - lo).astype(jnp.float32).reshape(shape)
    return (s / cnt).astype(x.dtype)


def _pool_mix(xb, rows, w, b, scale):
    bsz, t = xb.shape[0], xb.shape[1]
    outs = []
    for g, k in enumerate(POOL_WINDOWS):
        seg = xb[..., g * DG_B:(g + 1) * DG_B]
        if rows is None:
            pooled = _window_mean(seg, k, 1)
        else:
            sg = seg.reshape(bsz, rows, GRID_W, DG_B)
            pooled = _window_mean(_window_mean(sg, k, 2), k, 1).reshape(bsz, t, DG_B)
        outs.append(pooled - seg)
    p = jnp.stack(outs, axis=2)
    y = jnp.einsum('btgi,gij->btgj', p, w).reshape(bsz, t, W_B) + b
    return y * scale


def _spatial_gate(u, v, ln_g, ln_b, sg_w, sg_b):
    bsz, t = v.shape[0], v.shape[1]
    vn = _layernorm(v, ln_g, ln_b)
    vc = vn.reshape(bsz, t // CHUNK, CHUNK, G_C, DG_C)
    s = jnp.einsum('gpq,bnqgd->bnpgd', sg_w, vc) + jnp.swapaxes(sg_b, 0, 1)[:, :, None]
    return u * s.reshape(bsz, t, W_C)


def _branches_out(z, lru_out, rows, lp):
    y_a = (jax.nn.gelu(z[..., OFF_GA:OFF_B]) * lru_out) @ lp['out_a']
    y_b = _pool_mix(z[..., OFF_B:OFF_U], rows, lp['pool_w'], lp['pool_b'], lp['pool_scale']) @ lp['out_b']
    uv = jax.nn.gelu(z[..., OFF_U:OFF_G])
    y_c = _spatial_gate(uv[..., :W_C], uv[..., W_C:], lp['sg_ln_g'], lp['sg_ln_b'], lp['sg_w'], lp['sg_b']) @ lp['out_c']
    gates = jax.nn.sigmoid(z[..., OFF_G:])
    merged = (gates[..., :D_MODEL] * y_a + gates[..., D_MODEL:2 * D_MODEL] * y_b
              + gates[..., 2 * D_MODEL:] * y_c)
    return merged @ lp['w_o'] + lp['b_o']


def _mixer(h, hc, rows, lp, ctx_out):
    z = h @ lp['w_in'] + lp['b_in']
    if ctx_out:
        zc = hc @ lp['w_in'] + lp['b_in']
        xa_c = zc[..., :W_A]
    else:
        xa_c = hc @ lp['w_in'][:, :W_A] + lp['b_in'][:W_A]
    lru_lat, lru_ctx = _lru_branch(z[..., :W_A], xa_c, lp, ctx_out)
    y = _branches_out(z, lru_lat, rows, lp)
    yc = _branches_out(zc, lru_ctx, None, lp) if ctx_out else None
    return y, yc


def _moe_ffn(h, router_w, router_b, w_gate, b_gate, w_up, b_up, w_down, b_down):
    shp = h.shape
    x2 = h.reshape(-1, D_MODEL)
    n_tok = x2.shape[0]
    logits = (x2 @ router_w + router_b).astype(jnp.float32)
    top_vals, top_idx = lax.top_k(logits, TOP_K)
    probs = jax.nn.softmax(top_vals, axis=-1)
    n_assign = n_tok * TOP_K
    flat_e = top_idx.reshape(-1)
    order = jnp.argsort(flat_e)
    sorted_e = flat_e[order]
    sorted_tok = order // TOP_K
    counts = jnp.bincount(flat_e, length=N_EXPERTS)
    padded = (counts + MOE_BLOCK - 1) // MOE_BLOCK * MOE_BLOCK
    pad_end = jnp.cumsum(padded)
    pad_start = pad_end - padded
    start = jnp.cumsum(counts) - counts
    dest = pad_start[sorted_e] + jnp.arange(n_assign) - start[sorted_e]
    n_blocks = (n_assign + N_EXPERTS * (MOE_BLOCK - 1) + MOE_BLOCK - 1) // MOE_BLOCK
    n_rows = n_blocks * MOE_BLOCK
    row_tok = jnp.full((n_rows,), n_tok, jnp.int32).at[dest].set(sorted_tok.astype(jnp.int32))
    row_w = jnp.zeros((n_rows,), x2.dtype).at[dest].set(probs.reshape(-1)[order].astype(x2.dtype))
    block_e = jnp.minimum(jnp.searchsorted(pad_end, jnp.arange(n_blocks) * MOE_BLOCK, side='right'), N_EXPERTS - 1)
    x_ext = jnp.concatenate([x2, jnp.zeros((1, D_MODEL), x2.dtype)], axis=0)

    def expert_block(args):
        rows_i, e, rw = args
        xb = x_ext[rows_i]
        gate = jnp.minimum(xb @ w_gate[e] + b_gate[e], SWIGLU_LIMIT)
        up = jnp.clip(xb @ w_up[e] + b_up[e], -SWIGLU_LIMIT, SWIGLU_LIMIT)
        glu = gate * jax.nn.sigmoid(gate * SWIGLU_ALPHA)
        out = ((up + 1.0) * glu) @ w_down[e] + b_down[e]
        return out * rw[:, None]

    yp = lax.map(expert_block, (row_tok.reshape(n_blocks, MOE_BLOCK), block_e,
                                row_w.reshape(n_blocks, MOE_BLOCK))).reshape(n_rows, D_MODEL)
    y = jax.ops.segment_sum(yp, row_tok, num_segments=n_tok + 1)[:n_tok]
    return y.reshape(shp)


def setup_inputs(seed: int = 0) -> dict:
    key = jax.random.key(seed)
    ks = iter(jax.random.split(key, 64))
    f32 = jnp.float32
    L = DEPTH

    def nrm(shape, scale):
        return jax.random.normal(next(ks), shape, f32) * scale

    u = jax.random.uniform(next(ks), (L, 2, W_A), f32, 0.9, 0.999)
    base = u ** (1.0 / LRU_C)
    lru_lambda = jnp.log(base) - jnp.log1p(-base)
    return {
        'x': nrm((BATCH, SEQ, D_MODEL), 1.0),
        'c': nrm((BATCH, D_MODEL), 1.0),
        'ctx': nrm((BATCH, CTX_LEN, D_MODEL), 1.0),
        'c_ctx': nrm((D_MODEL,), 1.0),
        'ada_w': nrm((L, D_MODEL, 6 * D_MODEL), 0.5 * D_MODEL ** -0.5),
        'ada_b': nrm((L, 6 * D_MODEL), 0.02),
        'norm1_g': 1.0 + nrm((L, D_MODEL), 0.05),
        'norm2_g': 1.0 + nrm((L, D_MODEL), 0.05),
        'w_in': nrm((L, D_MODEL, IN_COLS), D_MODEL ** -0.5),
        'b_in': nrm((L, IN_COLS), 0.02),
        'conv_w': nrm((L, CONV_W, W_A), CONV_W ** -0.5),
        'conv_b': nrm((L, W_A), 0.02),
        'lru_wa': nrm((L, 2, H_A, DH_A, DH_A), DH_A ** -0.5),
        'lru_ba': nrm((L, 2, W_A), 0.02),
        'lru_wx': nrm((L, 2, H_A, DH_A, DH_A), DH_A ** -0.5),
        'lru_bx': nrm((L, 2, W_A), 0.02),
        'lru_lambda': lru_lambda,
        'out_a': nrm((L, W_A, D_MODEL), W_A ** -0.5),
        'pool_w': nrm((L, G_B, DG_B, DG_B), DG_B ** -0.5),
        'pool_b': nrm((L, W_B), 0.02),
        'pool_scale': 1.0 + nrm((L, W_B), 0.1),
        'out_b': nrm((L, W_B, D_MODEL), W_B ** -0.5),
        'sg_ln_g': 1.0 + nrm((L, W_C), 0.05),
        'sg_ln_b': nrm((L, W_C), 0.02),
        'sg_w': nrm((L, G_C, CHUNK, CHUNK), CHUNK ** -0.5),
        'sg_b': 1.0 + nrm((L, G_C, CHUNK), 0.02),
        'out_c': nrm((L, W_C, D_MODEL), W_C ** -0.5),
        'w_o': nrm((L, D_MODEL, D_MODEL), D_MODEL ** -0.5),
        'b_o': nrm((L, D_MODEL), 0.02),
        'router_w': nrm((L, D_MODEL, N_EXPERTS), D_MODEL ** -0.5),
        'router_b': nrm((L, N_EXPERTS), 0.01),
        'w_gate': nrm((L, N_EXPERTS, D_MODEL, D_FF), D_MODEL ** -0.5),
        'b_gate': nrm((L, N_EXPERTS, D_FF), 0.02),
        'w_up': nrm((L, N_EXPERTS, D_MODEL, D_FF), D_MODEL ** -0.5),
        'b_up': nrm((L, N_EXPERTS, D_FF), 0.02),
        'w_down': nrm((L, N_EXPERTS, D_FF, D_MODEL), D_FF ** -0.5),
        'b_down': nrm((L, N_EXPERTS, D_MODEL), 0.02),
        'final_g': 1.0 + nrm((D_MODEL,), 0.05),
    }


def reference(x, c, ctx, c_ctx, ada_w, ada_b, norm1_g, norm2_g, w_in, b_in, conv_w, conv_b,
              lru_wa, lru_ba, lru_wx, lru_bx, lru_lambda, out_a, pool_w, pool_b, pool_scale, out_b,
              sg_ln_g, sg_ln_b, sg_w, sg_b, out_c, w_o, b_o, router_w, router_b,
              w_gate, b_gate, w_up, b_up, w_down, b_down, final_g):
    rows = x.shape[1] // GRID_W
    n_ctx = ctx.shape[1]
    silu_c = jax.nn.silu(c)
    silu_cc = jax.nn.silu(c_ctx)
    for l in range(DEPTH):
        last = l == DEPTH - 1
        lp = {
            'w_in': w_in[l], 'b_in': b_in[l], 'conv_w': conv_w[l], 'conv_b': conv_b[l],
            'lru_wa': lru_wa[l], 'lru_ba': lru_ba[l], 'lru_wx': lru_wx[l], 'lru_bx': lru_bx[l],
            'lru_lambda': lru_lambda[l], 'out_a': out_a[l],
            'pool_w': pool_w[l], 'pool_b': pool_b[l], 'pool_scale': pool_scale[l], 'out_b': out_b[l],
            'sg_ln_g': sg_ln_g[l], 'sg_ln_b': sg_ln_b[l], 'sg_w': sg_w[l], 'sg_b': sg_b[l], 'out_c': out_c[l],
            'w_o': w_o[l], 'b_o': b_o[l],
        }
        mod = (silu_c @ ada_w[l] + ada_b[l])[:, None, :]
        sh1, sc1, g1, sh2, sc2, g2 = jnp.split(mod, 6, axis=-1)
        modc = silu_cc @ ada_w[l] + ada_b[l]
        sh1c, sc1c, g1c, sh2c, sc2c, g2c = jnp.split(modc, 6, axis=-1)

        h = _rmsnorm(x, norm1_g[l]) * (1.0 + sc1) + sh1
        hc = _rmsnorm(ctx, norm1_g[l]) * (1.0 + sc1c) + sh1c
        y, yc = _mixer(h, hc, rows, lp, not last)
        x = x + g1 * y
        h2 = _rmsnorm(x, norm2_g[l]) * (1.0 + sc2) + sh2
        moe_args = (router_w[l], router_b[l], w_gate[l], b_gate[l], w_up[l], b_up[l], w_down[l], b_down[l])
        if last:
            f = _moe_ffn(h2, *moe_args)
        else:
            ctx = ctx + g1c * yc
            h2c = _rmsnorm(ctx, norm2_g[l]) * (1.0 + sc2c) + sh2c
            both = _moe_ffn(jnp.concatenate([h2c, h2], axis=1), *moe_args)
            ctx = ctx + g2c * both[:, :n_ctx]
            f = both[:, n_ctx:]
        x = x + g2 * f
    return _rmsnorm(x, final_g)
```

```python
import functools

import jax
import jax.numpy as jnp
from jax import lax
from jax.experimental import pallas as pl
from jax.experimental.pallas import tpu as pltpu

F32 = jnp.float32
BF16 = jnp.bfloat16

GRID_W = 64
EPS = 1e-6
H_A = 8
CONV_W = 4
LRU_C = 8.0
POOL_WINDOWS = (2, 4, 8, 16)
CHUNK = 128
G_C = 4
N_BRANCH = 3
TOP_K = 4
SWIGLU_LIMIT = 7.0
SWIGLU_ALPHA = 1.702

LANES = 128
BS = 8
VMEM_LIMIT = 60 * 1024 * 1024

ROW_TILE = 512
LRU_ROWS = 512
MOE_TILE = 512
ROUTER_PAD = 128


def _cparams(sem):
    return pltpu.CompilerParams(dimension_semantics=sem, vmem_limit_bytes=VMEM_LIMIT)


def _sigmoid(v):
    return jax.nn.sigmoid(v)


def _rms(x, g):
    return x * lax.rsqrt(jnp.mean(x * x, axis=-1, keepdims=True) + EPS) * g


def _modulate(y, shift, scale):
    rows, d = y.shape
    y3 = y.reshape(rows // BS, BS, d)
    return (y3 * (1.0 + scale)[None] + shift[None]).reshape(rows, d)


def _mod_kernel(c_ref, w_ref, b_ref, o_ref):
    s = c_ref[...]
    s = s * _sigmoid(s)
    o_ref[...] = jnp.dot(s, w_ref[...], preferred_element_type=F32,
                         precision=lax.Precision.HIGHEST) + b_ref[...]


def _modulation(cc, ada_w, ada_b):
    depth, d, n6 = ada_w.shape
    r = cc.shape[0]
    tn = 1536
    return pl.pallas_call(
        _mod_kernel,
        grid=(depth, n6 // tn),
        in_specs=[pl.BlockSpec((r, d), lambda l, j: (0, 0)),
                  pl.BlockSpec((None, d, tn), lambda l, j: (l, 0, j)),
                  pl.BlockSpec((None, 1, tn), lambda l, j: (l, 0, j))],
        out_specs=pl.BlockSpec((None, r, tn), lambda l, j: (l, 0, j)),
        out_shape=jax.ShapeDtypeStruct((depth, r, n6), F32),
        compiler_params=_cparams(("arbitrary", "arbitrary")),
        name="adaln_modulation",
    )(cc, ada_w, ada_b.reshape(depth, 1, n6))


def _inproj_kernel(x_ref, mod_ref, g_ref, w_ref, b_ref, xa_ref, ga_ref, xb_ref, uv_ref, gt_ref):
    d = x_ref.shape[1]
    w_b = xb_ref.shape[1]
    h = _modulate(_rms(x_ref[...], g_ref[...]), mod_ref[0], mod_ref[1]).astype(BF16)

    def proj(lo, n):
        return jnp.dot(h, w_ref[:, lo:lo + n], preferred_element_type=F32) + b_ref[:, lo:lo + n]

    xa_ref[...] = proj(0, d)
    ga_ref[...] = jax.nn.gelu(proj(d, d)).astype(BF16)
    xb_ref[...] = proj(2 * d, w_b)
    off_u = 2 * d + w_b
    uv_ref[...] = jax.nn.gelu(proj(off_u, d)).astype(BF16)
    off_g = off_u + d
    for j in range(N_BRANCH):
        gt_ref[:, j * d:(j + 1) * d] = _sigmoid(proj(off_g + j * d, d)).astype(BF16)


def _inproj(x, modt, g, w, b, mod_idx):
    n, d = x.shape
    cols = w.shape[1]
    w_b = d // 2
    tm = ROW_TILE
    row = lambda i: (i, 0)
    const = lambda i: (0, 0)
    return pl.pallas_call(
        _inproj_kernel,
        grid=(n // tm,),
        in_specs=[pl.BlockSpec((tm, d), row),
                  pl.BlockSpec((2, None, BS, d), lambda i: (0, mod_idx(i), 0, 0)),
                  pl.BlockSpec((1, d), const),
                  pl.BlockSpec((d, cols), const),
                  pl.BlockSpec((1, cols), const)],
        out_specs=[pl.BlockSpec((tm, d), row), pl.BlockSpec((tm, d), row),
                   pl.BlockSpec((tm, w_b), row), pl.BlockSpec((tm, d), row),
                   pl.BlockSpec((tm, N_BRANCH * d), row)],
        out_shape=[jax.ShapeDtypeStruct((n, d), F32), jax.ShapeDtypeStruct((n, d), BF16),
                   jax.ShapeDtypeStruct((n, w_b), F32), jax.ShapeDtypeStruct((n, d), BF16),
                   jax.ShapeDtypeStruct((n, N_BRANCH * d), BF16)],
        compiler_params=_cparams(("arbitrary",)),
        name="in_projection",
    )(x, modt, g, w, b)


def _neg_expm1(y):
    series = -y * (1.0 + 0.5 * y * (1.0 + (1.0 / 3.0) * y * (1.0 + 0.25 * y)))
    return jnp.where(y > -0.01, series, 1.0 - jnp.exp(y))


def _lru_kernel(*refs, reverse, nc_ctx, nc_lat):
    if reverse:
        (xa_ref, xp_ref, xn_ref, cw_ref, cb_ref, wg_ref, bg_ref, lam_ref, hsf_ref, ga_ref,
         o_ref, xe, a_s, b_s, h_s) = refs
    else:
        (xa_ref, xp_ref, xn_ref, cw_ref, cb_ref, wg_ref, bg_ref, lam_ref,
         o_ref, xe, a_s, b_s, h_s) = refs
    rows, d = xa_ref.shape
    dh = d // H_A
    nt = rows // BS
    i = pl.program_id(1)
    is_ctx = i < nc_ctx
    if reverse:
        j = jnp.where(is_ctx, nc_ctx - 1 - i, nc_ctx + nc_lat - 1 - i)
    else:
        j = jnp.where(is_ctx, i, i - nc_ctx)
    n_seq = jnp.where(is_ctx, nc_ctx, nc_lat)

    @pl.when(i == 0)
    def _():
        h_s[...] = jnp.zeros_like(h_s)

    pad_lo = (CONV_W // 2) * BS
    xe[0:pad_lo, :] = jnp.where(j > 0, xp_ref[...], 0.0)
    xe[pad_lo:pad_lo + rows, :] = xa_ref[...]
    xe[pad_lo + rows:, :] = jnp.where(j < n_seq - 1, xn_ref[...], 0.0)

    lam = lam_ref[...]
    softplus_neg = jnp.maximum(-lam, 0.0) + jnp.log(1.0 + jnp.exp(-jnp.abs(lam)))
    coef = -LRU_C * softplus_neg

    for hd in range(H_A):
        cs = slice(hd * dh, (hd + 1) * dh)
        xc = cb_ref[:, cs] + cw_ref[0:1, cs] * xe[0:rows, cs]
        for tap in range(1, CONV_W):
            xc = xc + cw_ref[tap:tap + 1, cs] * xe[tap * BS:tap * BS + rows, cs]
        pre = jnp.dot(xc.astype(BF16), wg_ref[hd], preferred_element_type=F32) + bg_ref[hd]
        r = _sigmoid(pre[:, :dh])
        gi = _sigmoid(pre[:, dh:])
        log_a = r * coef[:, cs]
        a_s[:, cs] = jnp.exp(log_a)
        b_s[:, cs] = jnp.sqrt(_neg_expm1(2.0 * log_a)) * gi * xc

    def step(s, h):
        t = (nt - 1 - s) if reverse else s
        r0 = pl.multiple_of(t * BS, BS)
        h = a_s[pl.ds(r0, BS), :] * h + b_s[pl.ds(r0, BS), :]
        b_s[pl.ds(r0, BS), :] = h
        return h

    h_s[...] = lax.fori_loop(0, nt, step, h_s[...], unroll=8)

    if reverse:
        o_ref[...] = (ga_ref[...].astype(F32) * (hsf_ref[...] + b_s[...])).astype(BF16)
    else:
        o_ref[...] = b_s[...]


def _lru(xa, cw, cb, wg, bg, lam, geom, reverse, hsf=None, ga=None):
    n, d = xa.shape
    nbg, rl, rc = geom
    cr = LRU_ROWS
    nc_ctx, nc_lat = rc // cr, rl // cr
    pad_lo = (CONV_W // 2) * BS
    pad_hi = (CONV_W - 1 - CONV_W // 2) * BS

    def blk(bg_i, i):
        is_ctx = i < nc_ctx
        if reverse:
            j = jnp.where(is_ctx, nc_ctx - 1 - i, nc_ctx + nc_lat - 1 - i)
        else:
            j = jnp.where(is_ctx, i, i - nc_ctx)
        return jnp.where(is_ctx, (nbg * rl + bg_i * rc) // cr + j, bg_i * nc_lat + j)

    cur = lambda b, i: (blk(b, i), 0)
    prv = lambda b, i: (jnp.maximum(blk(b, i) * (cr // pad_lo) - 1, 0), 0)
    nxt = lambda b, i: (jnp.minimum((blk(b, i) + 1) * (cr // pad_hi), n // pad_hi - 1), 0)
    const2 = lambda b, i: (0, 0)
    const3 = lambda b, i: (0, 0, 0)
    in_specs = [pl.BlockSpec((cr, d), cur), pl.BlockSpec((pad_lo, d), prv), pl.BlockSpec((pad_hi, d), nxt),
                pl.BlockSpec((CONV_W, d), const2), pl.BlockSpec((1, d), const2),
                pl.BlockSpec((H_A, d // H_A, 2 * d // H_A), const3),
                pl.BlockSpec((H_A, 1, 2 * d // H_A), const3),
                pl.BlockSpec((1, d), const2)]
    args = [xa, xa, xa, cw, cb, wg, bg, lam]
    if reverse:
        in_specs += [pl.BlockSpec((cr, d), cur), pl.BlockSpec((cr, d), cur)]
        args += [hsf, ga]
    return pl.pallas_call(
        functools.partial(_lru_kernel, reverse=reverse, nc_ctx=nc_ctx, nc_lat=nc_lat),
        grid=(nbg, nc_ctx + nc_lat),
        in_specs=in_specs,
        out_specs=pl.BlockSpec((cr, d), cur),
        out_shape=jax.ShapeDtypeStruct((n, d), BF16 if reverse else F32),
        scratch_shapes=[pltpu.VMEM((pad_lo + cr + pad_hi, d), F32), pltpu.VMEM((cr, d), F32),
                        pltpu.VMEM((cr, d), F32), pltpu.VMEM((BS, d), F32)],
        compiler_params=_cparams(("arbitrary", "arbitrary")),
        name="rglru_reverse" if reverse else "rglru_forward",
    )(*args)


def _window_count(pos, k, n):
    return jnp.minimum(pos + (k - k // 2), n) - jnp.maximum(pos - k // 2, 0)


def _pool_kernel(*refs, n_lines, npix, aliased):
    if aliased:
        x_ref, _, w_ref, b_ref, sc_ref, o_ref, pad_s = refs
    else:
        x_ref, w_ref, b_ref, sc_ref, o_ref, pad_s = refs
    g = pl.program_id(1)
    n = npix * BS

    def variant(k):
        lo = (k // 2) * BS
        hi = (k - k // 2 - 1) * BS
        pad_s[0:lo, :] = jnp.zeros((lo, LANES), F32)
        if hi:
            pad_s[lo + n:lo + n + hi, :] = jnp.zeros((hi, LANES), F32)
        pix = lax.broadcasted_iota(jnp.int32, (n, LANES), 0) // BS
        cnt_pix = _window_count(pix, k, npix).astype(F32)

        def line_body(r, carry):
            base = pl.multiple_of(r * n, n)
            seg = x_ref[pl.ds(base, n), :]
            if n_lines > 1:
                acc = jnp.zeros((n, LANES), F32)
                for off in range(-(k // 2), k - k // 2):
                    rr = r + off
                    ok = jnp.logical_and(rr >= 0, rr < n_lines)
                    rc = jnp.clip(rr, 0, n_lines - 1)
                    src = x_ref[pl.ds(pl.multiple_of(rc * n, n), n), :]
                    acc = acc + jnp.where(ok, src, 0.0)
                line = acc / _window_count(r, k, n_lines).astype(F32)
            else:
                line = seg
            pad_s[lo:lo + n, :] = line
            acc2 = pad_s[0:n, :]
            for jj in range(1, k):
                acc2 = acc2 + pad_s[jj * BS:jj * BS + n, :]
            p = acc2 / cnt_pix - seg
            y = jnp.dot(p.astype(BF16), w_ref[...], preferred_element_type=F32) + b_ref[...]
            o_ref[pl.ds(base, n), :] = (y * sc_ref[...]).astype(BF16)
            return carry

        lax.fori_loop(0, n_lines, line_body, 0)

    for gi, k in enumerate(POOL_WINDOWS):
        pl.when(g == gi)(functools.partial(variant, k))


def _pool(xb, pw, pb, ps, geom, prev=None):
    n, w_b = xb.shape
    nbg, rl, rc = geom
    ng = len(POOL_WINDOWS)
    if prev is None:
        rows, n_lines, npix, base = rl, rl // (GRID_W * BS), GRID_W, 0
    else:
        rows, n_lines, npix, base = rc, 1, rc // BS, (nbg * rl) // rc
    kmax = max(POOL_WINDOWS)
    blk = lambda b, g: (base + b, g)
    gconst = lambda b, g: (0, g)
    aliased = prev is not None
    in_specs = [pl.BlockSpec((rows, LANES), blk)]
    args = [xb]
    if aliased:
        in_specs.append(pl.BlockSpec(memory_space=pl.ANY))
        args.append(prev)
    in_specs += [pl.BlockSpec((None, LANES, LANES), lambda b, g: (g, 0, 0)),
                 pl.BlockSpec((1, LANES), gconst), pl.BlockSpec((1, LANES), gconst)]
    args += [pw, pb, ps]
    return pl.pallas_call(
        functools.partial(_pool_kernel, n_lines=n_lines, npix=npix, aliased=aliased),
        grid=(nbg, ng),
        in_specs=in_specs,
        out_specs=pl.BlockSpec((rows, LANES), blk),
        out_shape=jax.ShapeDtypeStruct((n, w_b), BF16),
        scratch_shapes=[pltpu.VMEM(((npix + kmax) * BS, LANES), F32)],
        input_output_aliases={1: 0} if aliased else {},
        compiler_params=_cparams(("arbitrary", "arbitrary")),
        name="pool_context" if aliased else "pool_latent",
    )(*args)


def _sgu_kernel(uv_ref, lng_ref, lnb_ref, wk_ref, bias_ref, o_ref):
    w_c = o_ref.shape[1]
    dg = w_c // G_C
    v = uv_ref[:, w_c:].astype(F32)
    mu = jnp.mean(v, axis=-1, keepdims=True)
    vc = v - mu
    var = jnp.mean(vc * vc, axis=-1, keepdims=True)
    vn = (vc * lax.rsqrt(var + EPS) * lng_ref[...] + lnb_ref[...]).astype(BF16)
    for g in range(G_C):
        cs = slice(g * dg, (g + 1) * dg)
        s = jnp.dot(wk_ref[g], vn[:, cs], preferred_element_type=F32) + bias_ref[:, cs]
        o_ref[:, cs] = (uv_ref[:, cs].astype(F32) * s).astype(BF16)


def _sgu(uv, lng, lnb, wk, bias):
    n, d = uv.shape
    w_c = d // 2
    rows = CHUNK * BS
    return pl.pallas_call(
        _sgu_kernel,
        grid=(n // rows,),
        in_specs=[pl.BlockSpec((rows, d), lambda i: (i, 0)),
                  pl.BlockSpec((1, w_c), lambda i: (0, 0)), pl.BlockSpec((1, w_c), lambda i: (0, 0)),
                  pl.BlockSpec((G_C, rows, rows), lambda i: (0, 0, 0)),
                  pl.BlockSpec((rows, w_c), lambda i: (0, 0))],
        out_specs=pl.BlockSpec((rows, w_c), lambda i: (i, 0)),
        out_shape=jax.ShapeDtypeStruct((n, w_c), BF16),
        compiler_params=_cparams(("arbitrary",)),
        name="spatial_gate",
    )(uv, lng, lnb, wk, bias)


def _merge_kernel(x_ref, la_ref, pb_ref, sg_ref, gt_ref, mod_ref, oa_ref, ob_ref, oc_ref, wo_ref, bo_ref,
                  g2_ref, rw_ref, rb_ref, xo_ref, h2_ref, lg_ref):
    d = x_ref.shape[1]
    y_a = jnp.dot(la_ref[...], oa_ref[...], preferred_element_type=F32)
    y_b = jnp.dot(pb_ref[...], ob_ref[...], preferred_element_type=F32)
    y_c = jnp.dot(sg_ref[...], oc_ref[...], preferred_element_type=F32)
    merged = (gt_ref[:, 0:d].astype(F32) * y_a + gt_ref[:, d:2 * d].astype(F32) * y_b
              + gt_ref[:, 2 * d:3 * d].astype(F32) * y_c)
    y = jnp.dot(merged.astype(BF16), wo_ref[...], preferred_element_type=F32) + bo_ref[...]
    rows = y.shape[0]
    gate1 = mod_ref[0]
    xn = x_ref[...] + (y.reshape(rows // BS, BS, d) * gate1[None]).reshape(rows, d)
    xo_ref[...] = xn
    h2 = _modulate(_rms(xn, g2_ref[...]), mod_ref[1], mod_ref[2])
    h2_ref[...] = h2.astype(BF16)
    lg_ref[...] = jnp.dot(h2, rw_ref[...], preferred_element_type=F32,
                          precision=lax.Precision.HIGHEST) + rb_ref[...]


def _merge(x, la, pb, sg, gt, modt, oa, ob, oc, wo, bo, g2, rw, rb, mod_idx):
    n, d = x.shape
    w_b = pb.shape[1]
    tm = ROW_TILE
    row = lambda i: (i, 0)
    const = lambda i: (0, 0)
    return pl.pallas_call(
        _merge_kernel,
        grid=(n // tm,),
        in_specs=[pl.BlockSpec((tm, d), row), pl.BlockSpec((tm, d), row), pl.BlockSpec((tm, w_b), row),
                  pl.BlockSpec((tm, w_b), row), pl.BlockSpec((tm, N_BRANCH * d), row),
                  pl.BlockSpec((3, None, BS, d), lambda i: (0, mod_idx(i), 0, 0)),
                  pl.BlockSpec((d, d), const), pl.BlockSpec((w_b, d), const), pl.BlockSpec((w_b, d), const),
                  pl.BlockSpec((d, d), const), pl.BlockSpec((1, d), const), pl.BlockSpec((1, d), const),
                  pl.BlockSpec((d, ROUTER_PAD), const), pl.BlockSpec((1, ROUTER_PAD), const)],
        out_specs=[pl.BlockSpec((tm, d), row), pl.BlockSpec((tm, d), row), pl.BlockSpec((tm, ROUTER_PAD), row)],
        out_shape=[jax.ShapeDtypeStruct((n, d), F32), jax.ShapeDtypeStruct((n, d), BF16),
                   jax.ShapeDtypeStruct((n, ROUTER_PAD), F32)],
        compiler_params=_cparams(("arbitrary",)),
        name="merge_out_projection",
    )(x, la, pb, sg, gt, modt, oa, ob, oc, wo, bo, g2, rw, rb)


def _expert_kernel(be_ref, x_ref, wg_ref, bg_ref, wu_ref, bu_ref, wd_ref, bd_ref, o_ref):
    del be_ref
    xb = x_ref[...]
    gate = jnp.minimum(jnp.dot(xb, wg_ref[...], preferred_element_type=F32) + bg_ref[...], SWIGLU_LIMIT)
    up = jnp.clip(jnp.dot(xb, wu_ref[...], preferred_element_type=F32) + bu_ref[...],
                  -SWIGLU_LIMIT, SWIGLU_LIMIT)
    glu = gate * _sigmoid(gate * SWIGLU_ALPHA)
    act = ((up + 1.0) * glu).astype(BF16)
    o_ref[...] = (jnp.dot(act, wd_ref[...], preferred_element_type=F32) + bd_ref[...]).astype(BF16)


def _experts(block_e, xg, wg, bg, wu, bu, wd, bd):
    n_rows, d = xg.shape
    dff = wg.shape[2]
    tm = MOE_TILE
    wspec = lambda r, c: pl.BlockSpec((None, r, c), lambda i, be: (be[i], 0, 0))
    return pl.pallas_call(
        _expert_kernel,
        grid_spec=pltpu.PrefetchScalarGridSpec(
            num_scalar_prefetch=1,
            grid=(n_rows // tm,),
            in_specs=[pl.BlockSpec((tm, d), lambda i, be: (i, 0)),
                      wspec(d, dff), wspec(1, dff), wspec(d, dff), wspec(1, dff), wspec(dff, d), wspec(1, d)],
            out_specs=pl.BlockSpec((tm, d), lambda i, be: (i, 0))),
        out_shape=jax.ShapeDtypeStruct((n_rows, d), BF16),
        compiler_params=_cparams(("arbitrary",)),
        name="moe_experts",
    )(block_e, xg, wg, bg, wu, bu, wd, bd)


def _route(logits, n_experts):
    n = logits.shape[0]
    tm = MOE_TILE
    top_vals, top_idx = lax.top_k(logits, TOP_K)
    probs = jax.nn.softmax(top_vals, axis=-1)
    flat_e = top_idx.reshape(-1)
    onehot = (flat_e[:, None] == jnp.arange(n_experts, dtype=jnp.int32)[None, :]).astype(jnp.int32)
    csum = jnp.cumsum(onehot, axis=0)
    rank = jnp.take_along_axis(csum, flat_e[:, None], axis=1)[:, 0] - 1
    counts = csum[-1]
    padded = (counts + tm - 1) // tm * tm
    pad_end = jnp.cumsum(padded)
    pad_start = pad_end - padded
    dest = (pad_start[flat_e] + rank).astype(jnp.int32)
    n_assign = n * TOP_K
    n_blocks = (n_assign + n_experts * (tm - 1) + tm - 1) // tm
    row_tok = jnp.zeros((n_blocks * tm,), jnp.int32).at[dest].set(
        jnp.arange(n_assign, dtype=jnp.int32) // TOP_K)
    block_e = jnp.minimum(
        jnp.searchsorted(pad_end, jnp.arange(n_blocks, dtype=jnp.int32) * tm, side='right'),
        n_experts - 1).astype(jnp.int32)
    return probs, dest.reshape(n, TOP_K), row_tok, block_e


def _final_kernel(x_ref, g_ref, o_ref):
    o_ref[...] = _rms(x_ref[...], g_ref[...])


def _final_norm(x, g, n_rows):
    d = x.shape[1]
    tm = ROW_TILE
    return pl.pallas_call(
        _final_kernel,
        grid=(n_rows // tm,),
        in_specs=[pl.BlockSpec((tm, d), lambda i: (i, 0)), pl.BlockSpec((1, d), lambda i: (0, 0))],
        out_specs=pl.BlockSpec((tm, d), lambda i: (i, 0)),
        out_shape=jax.ShapeDtypeStruct((n_rows, d), F32),
        compiler_params=_cparams(("arbitrary",)),
        name="final_rmsnorm",
    )(x, g)


def _to_rows(a):
    b, t, d = a.shape
    return a.reshape(b // BS, BS, t, d).transpose(0, 2, 1, 3).reshape(b * t, d)


def _from_rows(r, b, t):
    d = r.shape[1]
    return r.reshape(b // BS, t, BS, d).transpose(0, 2, 1, 3).reshape(b, t, d)


def kernel(x, c, ctx, c_ctx, ada_w, ada_b, norm1_g, norm2_g, w_in, b_in, conv_w, conv_b, lru_wa, lru_ba, lru_wx, lru_bx, lru_lambda, out_a, pool_w, pool_b, pool_scale, out_b, sg_ln_g, sg_ln_b, sg_w, sg_b, out_c, w_o, b_o, router_w, router_b, w_gate, b_gate, w_up, b_up, w_down, b_down, final_g):
    bsz, seq, d = x.shape
    n_ctx = ctx.shape[1]
    depth = ada_w.shape[0]
    n_experts = router_w.shape[2]
    nbg = bsz // BS
    rl, rc = seq * BS, n_ctx * BS
    geom = (nbg, rl, rc)
    n_lat = nbg * rl
    assert bsz % BS == 0 and seq % GRID_W == 0 and rc % ROW_TILE == 0 and rl % rc == 0
    assert rc % LRU_ROWS == 0 and rc % (CHUNK * BS) == 0 and d // H_A == LANES

    tiles_lat_bg = rl // ROW_TILE
    n_lat_tiles = nbg * tiles_lat_bg
    mod_idx = lambda i: jnp.where(i < n_lat_tiles, i // tiles_lat_bg, nbg)

    xs = jnp.concatenate([_to_rows(x), _to_rows(ctx)], axis=0)

    cc = jnp.concatenate([c, c_ctx[None], jnp.zeros((BS - 1, d), F32)], axis=0)
    mod = _modulation(cc, ada_w, ada_b)
    mod = mod.reshape(depth, bsz + BS, 6, d).transpose(0, 2, 1, 3)
    mod_lat = mod[:, :, :bsz].reshape(depth, 6, nbg, BS, d)
    mod_ctx = jnp.broadcast_to(mod[:, :, bsz:bsz + 1, None, :], (depth, 6, 1, BS, d))
    modt = jnp.concatenate([mod_lat, mod_ctx], axis=2)

    eye = jnp.eye(BS, dtype=F32)
    for l in range(depth):
        m = modt[l]
        w_in_b = w_in[l].astype(BF16)
        xa, ga, xb, uv, gt = _inproj(xs, m[0:2], norm1_g[l][None], w_in_b, b_in[l][None], mod_idx)

        hsf = None
        for dr in range(2):
            wgate = jnp.concatenate([lru_wa[l, dr], lru_wx[l, dr]], axis=-1).astype(BF16)
            dh = d // H_A
            bgate = jnp.concatenate([lru_ba[l, dr].reshape(H_A, 1, dh), lru_bx[l, dr].reshape(H_A, 1, dh)], axis=-1)
            args = (xa, conv_w[l], conv_b[l][None], wgate, bgate, lru_lambda[l, dr][None], geom)
            if dr == 0:
                hsf = _lru(*args, reverse=False)
            else:
                la = _lru(*args, reverse=True, hsf=hsf, ga=ga)

        pw = pool_w[l].astype(BF16)
        pbm = _pool(xb, pw, pool_b[l][None], pool_scale[l][None], geom)
        pbm = _pool(xb, pw, pool_b[l][None], pool_scale[l][None], geom, prev=pbm)

        wk = jnp.einsum('gpq,bc->gpbqc', sg_w[l], eye).reshape(G_C, CHUNK * BS, CHUNK * BS).astype(BF16)
        w_c = d // 2
        sbias = jnp.broadcast_to(sg_b[l].T[:, None, :, None], (CHUNK, BS, G_C, w_c // G_C)).reshape(CHUNK * BS, w_c)
        sgo = _sgu(uv, sg_ln_g[l][None], sg_ln_b[l][None], wk, sbias)

        rw = jnp.pad(router_w[l], ((0, 0), (0, ROUTER_PAD - n_experts)))
        rb = jnp.pad(router_b[l], (0, ROUTER_PAD - n_experts))[None]
        x_mid, h2, logits = _merge(xs, la, pbm, sgo, gt, m[2:5], out_a[l].astype(BF16), out_b[l].astype(BF16),
                                   out_c[l].astype(BF16), w_o[l].astype(BF16), b_o[l][None], norm2_g[l][None],
                                   rw, rb, mod_idx)

        last = l == depth - 1
        n_tok = n_lat if last else xs.shape[0]
        probs, dest, row_tok, block_e = _route(logits[:n_tok, :n_experts], n_experts)
        xg = jnp.take(h2, row_tok, axis=0)
        yp = _experts(block_e, xg, w_gate[l].astype(BF16), b_gate[l][:, None], w_up[l].astype(BF16),
                      b_up[l][:, None], w_down[l].astype(BF16), b_down[l][:, None])
        f = jnp.einsum('nk,nkd->nd', probs, jnp.take(yp, dest, axis=0).astype(F32))
        g2 = m[5]
        g2_lat = jnp.broadcast_to(g2[:nbg, None], (nbg, seq, BS, d)).reshape(n_lat, d)
        if last:
            xs = x_mid[:n_lat] + g2_lat * f
        else:
            g2_ctx = jnp.broadcast_to(g2[nbg:, None], (nbg, n_ctx, BS, d)).reshape(nbg * rc, d)
            xs = x_mid + jnp.concatenate([g2_lat, g2_ctx], axis=0) * f

    out = _final_norm(xs, final_g[None], n_lat)
    return _from_rows(out, bsz, seq)
```

```python
import functools

import jax
import jax.numpy as jnp
from jax import lax
from jax.experimental import pallas as pl
from jax.experimental.pallas import tpu as pltpu

F32 = jnp.float32
BF16 = jnp.bfloat16

GRID_W = 64
EPS = 1e-6
H_A = 8
CONV_W = 4
LRU_C = 8.0
POOL_WINDOWS = (2, 4, 8, 16)
CHUNK = 128
G_C = 4
N_BRANCH = 3
TOP_K = 4
SWIGLU_LIMIT = 7.0
SWIGLU_ALPHA = 1.702

LANES = 128
BS = 8
VMEM_LIMIT = 60 * 1024 * 1024

ROW_TILE = 512
LRU_ROWS = 512
MOE_TILE = 512
ROUTER_PAD = 128
ROUTER_PAD_LOGIT = -1e30


def _cparams(sem):
    return pltpu.CompilerParams(dimension_semantics=sem, vmem_limit_bytes=VMEM_LIMIT)


def _sigmoid(v):
    return jax.nn.sigmoid(v)


def _rms(x, g):
    return x * lax.rsqrt(jnp.mean(x * x, axis=-1, keepdims=True) + EPS) * g


def _modulate(y, shift, scale):
    rows, d = y.shape
    y3 = y.reshape(rows // BS, BS, d)
    return (y3 * (1.0 + scale)[None] + shift[None]).reshape(rows, d)


def _mod_kernel(c_ref, w_ref, b_ref, o_ref):
    s = c_ref[...]
    s = s * _sigmoid(s)
    o_ref[...] = jnp.dot(s, w_ref[...], preferred_element_type=F32,
                         precision=lax.Precision.HIGHEST) + b_ref[...]


def _modulation(cc, ada_w, ada_b):
    depth, d, n6 = ada_w.shape
    r = cc.shape[0]
    tn = 1536
    return pl.pallas_call(
        _mod_kernel,
        grid=(depth, n6 // tn),
        in_specs=[pl.BlockSpec((r, d), lambda l, j: (0, 0)),
                  pl.BlockSpec((None, d, tn), lambda l, j: (l, 0, j)),
                  pl.BlockSpec((None, 1, tn), lambda l, j: (l, 0, j))],
        out_specs=pl.BlockSpec((None, r, tn), lambda l, j: (l, 0, j)),
        out_shape=jax.ShapeDtypeStruct((depth, r, n6), F32),
        compiler_params=_cparams(("arbitrary", "arbitrary")),
        name="adaln_modulation",
    )(cc, ada_w, ada_b.reshape(depth, 1, n6))


def _inproj_kernel(x_ref, mod_ref, g_ref, w_ref, b_ref, xa_ref, ga_ref, xb_ref, uv_ref, gt_ref):
    d = x_ref.shape[1]
    w_b = xb_ref.shape[1]
    h = _modulate(_rms(x_ref[...], g_ref[...]), mod_ref[0], mod_ref[1]).astype(BF16)

    def proj(lo, n):
        return jnp.dot(h, w_ref[:, lo:lo + n], preferred_element_type=F32) + b_ref[:, lo:lo + n]

    xa_ref[...] = proj(0, d)
    ga_ref[...] = jax.nn.gelu(proj(d, d)).astype(BF16)
    xb_ref[...] = proj(2 * d, w_b)
    off_u = 2 * d + w_b
    uv_ref[...] = jax.nn.gelu(proj(off_u, d)).astype(BF16)
    off_g = off_u + d
    for j in range(N_BRANCH):
        gt_ref[:, j * d:(j + 1) * d] = _sigmoid(proj(off_g + j * d, d)).astype(BF16)


def _inproj(x, modt, g, w, b, mod_idx):
    n, d = x.shape
    cols = w.shape[1]
    w_b = d // 2
    tm = ROW_TILE
    row = lambda i: (i, 0)
    const = lambda i: (0, 0)
    return pl.pallas_call(
        _inproj_kernel,
        grid=(n // tm,),
        in_specs=[pl.BlockSpec((tm, d), row),
                  pl.BlockSpec((2, None, BS, d), lambda i: (0, mod_idx(i), 0, 0)),
                  pl.BlockSpec((1, d), const),
                  pl.BlockSpec((d, cols), const),
                  pl.BlockSpec((1, cols), const)],
        out_specs=[pl.BlockSpec((tm, d), row), pl.BlockSpec((tm, d), row),
                   pl.BlockSpec((tm, w_b), row), pl.BlockSpec((tm, d), row),
                   pl.BlockSpec((tm, N_BRANCH * d), row)],
        out_shape=[jax.ShapeDtypeStruct((n, d), F32), jax.ShapeDtypeStruct((n, d), BF16),
                   jax.ShapeDtypeStruct((n, w_b), F32), jax.ShapeDtypeStruct((n, d), BF16),
                   jax.ShapeDtypeStruct((n, N_BRANCH * d), BF16)],
        compiler_params=_cparams(("arbitrary",)),
        name="in_projection",
    )(x, modt, g, w, b)


def _neg_expm1(y):
    series = -y * (1.0 + 0.5 * y * (1.0 + (1.0 / 3.0) * y * (1.0 + 0.25 * y)))
    return jnp.where(y > -0.01, series, 1.0 - jnp.exp(y))


def _lru_kernel(*refs, reverse, nc_ctx, nc_lat):
    if reverse:
        (xa_ref, xp_ref, xn_ref, cw_ref, cb_ref, wg_ref, bg_ref, lam_ref, hsf_ref, ga_ref,
         o_ref, xe, a_s, b_s, h_s) = refs
    else:
        (xa_ref, xp_ref, xn_ref, cw_ref, cb_ref, wg_ref, bg_ref, lam_ref,
         o_ref, xe, a_s, b_s, h_s) = refs
    rows, d = xa_ref.shape
    dh = d // H_A
    nt = rows // BS
    i = pl.program_id(1)
    is_ctx = i < nc_ctx
    if reverse:
        j = jnp.where(is_ctx, nc_ctx - 1 - i, nc_ctx + nc_lat - 1 - i)
    else:
        j = jnp.where(is_ctx, i, i - nc_ctx)
    n_seq = jnp.where(is_ctx, nc_ctx, nc_lat)

    @pl.when(i == 0)
    def _():
        h_s[...] = jnp.zeros_like(h_s)

    pad_lo = (CONV_W // 2) * BS
    xe[0:pad_lo, :] = jnp.where(j > 0, xp_ref[...], 0.0)
    xe[pad_lo:pad_lo + rows, :] = xa_ref[...]
    xe[pad_lo + rows:, :] = jnp.where(j < n_seq - 1, xn_ref[...], 0.0)

    lam = lam_ref[...]
    softplus_neg = jnp.maximum(-lam, 0.0) + jnp.log(1.0 + jnp.exp(-jnp.abs(lam)))
    coef = -LRU_C * softplus_neg

    for hd in range(H_A):
        cs = slice(hd * dh, (hd + 1) * dh)
        xc = cb_ref[:, cs] + cw_ref[0:1, cs] * xe[0:rows, cs]
        for tap in range(1, CONV_W):
            xc = xc + cw_ref[tap:tap + 1, cs] * xe[tap * BS:tap * BS + rows, cs]
        pre = jnp.dot(xc.astype(BF16), wg_ref[hd], preferred_element_type=F32) + bg_ref[hd]
        r = _sigmoid(pre[:, :dh])
        gi = _sigmoid(pre[:, dh:])
        log_a = r * coef[:, cs]
        a_s[:, cs] = jnp.exp(log_a)
        b_s[:, cs] = jnp.sqrt(_neg_expm1(2.0 * log_a)) * gi * xc

    def step(s, h):
        t = (nt - 1 - s) if reverse else s
        r0 = pl.multiple_of(t * BS, BS)
        h = a_s[pl.ds(r0, BS), :] * h + b_s[pl.ds(r0, BS), :]
        b_s[pl.ds(r0, BS), :] = h
        return h

    h_s[...] = lax.fori_loop(0, nt, step, h_s[...], unroll=8)

    if reverse:
        o_ref[...] = (ga_ref[...].astype(F32) * (hsf_ref[...] + b_s[...])).astype(BF16)
    else:
        o_ref[...] = b_s[...]


def _lru(xa, cw, cb, wg, bg, lam, geom, reverse, hsf=None, ga=None):
    n, d = xa.shape
    nbg, rl, rc = geom
    cr = LRU_ROWS
    nc_ctx, nc_lat = rc // cr, rl // cr
    pad_lo = (CONV_W // 2) * BS
    pad_hi = (CONV_W - 1 - CONV_W // 2) * BS

    def blk(bg_i, i):
        is_ctx = i < nc_ctx
        if reverse:
            j = jnp.where(is_ctx, nc_ctx - 1 - i, nc_ctx + nc_lat - 1 - i)
        else:
            j = jnp.where(is_ctx, i, i - nc_ctx)
        return jnp.where(is_ctx, (nbg * rl + bg_i * rc) // cr + j, bg_i * nc_lat + j)

    cur = lambda b, i: (blk(b, i), 0)
    prv = lambda b, i: (jnp.maximum(blk(b, i) * (cr // pad_lo) - 1, 0), 0)
    nxt = lambda b, i: (jnp.minimum((blk(b, i) + 1) * (cr // pad_hi), n // pad_hi - 1), 0)
    const2 = lambda b, i: (0, 0)
    const3 = lambda b, i: (0, 0, 0)
    in_specs = [pl.BlockSpec((cr, d), cur), pl.BlockSpec((pad_lo, d), prv), pl.BlockSpec((pad_hi, d), nxt),
                pl.BlockSpec((CONV_W, d), const2), pl.BlockSpec((1, d), const2),
                pl.BlockSpec((H_A, d // H_A, 2 * d // H_A), const3),
                pl.BlockSpec((H_A, 1, 2 * d // H_A), const3),
                pl.BlockSpec((1, d), const2)]
    args = [xa, xa, xa, cw, cb, wg, bg, lam]
    if reverse:
        in_specs += [pl.BlockSpec((cr, d), cur), pl.BlockSpec((cr, d), cur)]
        args += [hsf, ga]
    return pl.pallas_call(
        functools.partial(_lru_kernel, reverse=reverse, nc_ctx=nc_ctx, nc_lat=nc_lat),
        grid=(nbg, nc_ctx + nc_lat),
        in_specs=in_specs,
        out_specs=pl.BlockSpec((cr, d), cur),
        out_shape=jax.ShapeDtypeStruct((n, d), BF16 if reverse else F32),
        scratch_shapes=[pltpu.VMEM((pad_lo + cr + pad_hi, d), F32), pltpu.VMEM((cr, d), F32),
                        pltpu.VMEM((cr, d), F32), pltpu.VMEM((BS, d), F32)],
        compiler_params=_cparams(("arbitrary", "arbitrary")),
        name="rglru_reverse" if reverse else "rglru_forward",
    )(*args)


def _window_count(pos, k, n):
    return jnp.minimum(pos + (k - k // 2), n) - jnp.maximum(pos - k // 2, 0)


def _pool_kernel(*refs, n_lines, npix, aliased):
    if aliased:
        x_ref, _, w_ref, b_ref, sc_ref, o_ref, pad_s = refs
    else:
        x_ref, w_ref, b_ref, sc_ref, o_ref, pad_s = refs
    g = pl.program_id(1)
    n = npix * BS

    def variant(k):
        lo = (k // 2) * BS
        hi = (k - k // 2 - 1) * BS
        pad_s[0:lo, :] = jnp.zeros((lo, LANES), F32)
        if hi:
            pad_s[lo + n:lo + n + hi, :] = jnp.zeros((hi, LANES), F32)
        pix = lax.broadcasted_iota(jnp.int32, (n, LANES), 0) // BS
        cnt_pix = _window_count(pix, k, npix).astype(F32)

        def line_body(r, carry):
            base = pl.multiple_of(r * n, n)
            seg = x_ref[pl.ds(base, n), :]
            if n_lines > 1:
                acc = jnp.zeros((n, LANES), F32)
                for off in range(-(k // 2), k - k // 2):
                    rr = r + off
                    ok = jnp.logical_and(rr >= 0, rr < n_lines)
                    rc = jnp.clip(rr, 0, n_lines - 1)
                    src = x_ref[pl.ds(pl.multiple_of(rc * n, n), n), :]
                    acc = acc + jnp.where(ok, src, 0.0)
                line = acc / _window_count(r, k, n_lines).astype(F32)
            else:
                line = seg
            pad_s[lo:lo + n, :] = line
            acc2 = pad_s[0:n, :]
            for jj in range(1, k):
                acc2 = acc2 + pad_s[jj * BS:jj * BS + n, :]
            p = acc2 / cnt_pix - seg
            y = jnp.dot(p.astype(BF16), w_ref[...], preferred_element_type=F32) + b_ref[...]
            o_ref[pl.ds(base, n), :] = (y * sc_ref[...]).astype(BF16)
            return carry

        lax.fori_loop(0, n_lines, line_body, 0)

    for gi, k in enumerate(POOL_WINDOWS):
        pl.when(g == gi)(functools.partial(variant, k))


def _pool(xb, pw, pb, ps, geom, prev=None):
    n, w_b = xb.shape
    nbg, rl, rc = geom
    ng = len(POOL_WINDOWS)
    if prev is None:
        rows, n_lines, npix, base = rl, rl // (GRID_W * BS), GRID_W, 0
    else:
        rows, n_lines, npix, base = rc, 1, rc // BS, (nbg * rl) // rc
    kmax = max(POOL_WINDOWS)
    blk = lambda b, g: (base + b, g)
    gconst = lambda b, g: (0, g)
    aliased = prev is not None
    in_specs = [pl.BlockSpec((rows, LANES), blk)]
    args = [xb]
    if aliased:
        in_specs.append(pl.BlockSpec(memory_space=pl.ANY))
        args.append(prev)
    in_specs += [pl.BlockSpec((None, LANES, LANES), lambda b, g: (g, 0, 0)),
                 pl.BlockSpec((1, LANES), gconst), pl.BlockSpec((1, LANES), gconst)]
    args += [pw, pb, ps]
    return pl.pallas_call(
        functools.partial(_pool_kernel, n_lines=n_lines, npix=npix, aliased=aliased),
        grid=(nbg, ng),
        in_specs=in_specs,
        out_specs=pl.BlockSpec((rows, LANES), blk),
        out_shape=jax.ShapeDtypeStruct((n, w_b), BF16),
        scratch_shapes=[pltpu.VMEM(((npix + kmax) * BS, LANES), F32)],
        input_output_aliases={1: 0} if aliased else {},
        compiler_params=_cparams(("arbitrary", "arbitrary")),
        name="pool_context" if aliased else "pool_latent",
    )(*args)


def _sgu_kernel(uv_ref, lng_ref, lnb_ref, wk_ref, bias_ref, o_ref):
    w_c = o_ref.shape[1]
    dg = w_c // G_C
    v = uv_ref[:, w_c:].astype(F32)
    mu = jnp.mean(v, axis=-1, keepdims=True)
    vc = v - mu
    var = jnp.mean(vc * vc, axis=-1, keepdims=True)
    vn = (vc * lax.rsqrt(var + EPS) * lng_ref[...] + lnb_ref[...]).astype(BF16)
    for g in range(G_C):
        cs = slice(g * dg, (g + 1) * dg)
        s = jnp.dot(wk_ref[g], vn[:, cs], preferred_element_type=F32) + bias_ref[:, cs]
        o_ref[:, cs] = (uv_ref[:, cs].astype(F32) * s).astype(BF16)


def _sgu(uv, lng, lnb, wk, bias):
    n, d = uv.shape
    w_c = d // 2
    rows = CHUNK * BS
    return pl.pallas_call(
        _sgu_kernel,
        grid=(n // rows,),
        in_specs=[pl.BlockSpec((rows, d), lambda i: (i, 0)),
                  pl.BlockSpec((1, w_c), lambda i: (0, 0)), pl.BlockSpec((1, w_c), lambda i: (0, 0)),
                  pl.BlockSpec((G_C, rows, rows), lambda i: (0, 0, 0)),
                  pl.BlockSpec((rows, w_c), lambda i: (0, 0))],
        out_specs=pl.BlockSpec((rows, w_c), lambda i: (i, 0)),
        out_shape=jax.ShapeDtypeStruct((n, w_c), BF16),
        compiler_params=_cparams(("arbitrary",)),
        name="spatial_gate",
    )(uv, lng, lnb, wk, bias)


def _merge_kernel(x_ref, la_ref, pb_ref, sg_ref, gt_ref, mod_ref, oa_ref, ob_ref, oc_ref, wo_ref, bo_ref,
                  g2_ref, rw_ref, rb_ref, xo_ref, h2_ref, lg_ref):
    d = x_ref.shape[1]
    y_a = jnp.dot(la_ref[...], oa_ref[...], preferred_element_type=F32)
    y_b = jnp.dot(pb_ref[...], ob_ref[...], preferred_element_type=F32)
    y_c = jnp.dot(sg_ref[...], oc_ref[...], preferred_element_type=F32)
    merged = (gt_ref[:, 0:d].astype(F32) * y_a + gt_ref[:, d:2 * d].astype(F32) * y_b
              + gt_ref[:, 2 * d:3 * d].astype(F32) * y_c)
    y = jnp.dot(merged.astype(BF16), wo_ref[...], preferred_element_type=F32) + bo_ref[...]
    rows = y.shape[0]
    gate1 = mod_ref[0]
    xn = x_ref[...] + (y.reshape(rows // BS, BS, d) * gate1[None]).reshape(rows, d)
    xo_ref[...] = xn
    h2 = _modulate(_rms(xn, g2_ref[...]), mod_ref[1], mod_ref[2])
    h2_ref[...] = h2.astype(BF16)
    lg_ref[...] = jnp.dot(h2, rw_ref[...], preferred_element_type=F32,
                          precision=lax.Precision.HIGHEST) + rb_ref[...]


def _merge(x, la, pb, sg, gt, modt, oa, ob, oc, wo, bo, g2, rw, rb, mod_idx):
    n, d = x.shape
    w_b = pb.shape[1]
    tm = ROW_TILE
    row = lambda i: (i, 0)
    const = lambda i: (0, 0)
    return pl.pallas_call(
        _merge_kernel,
        grid=(n // tm,),
        in_specs=[pl.BlockSpec((tm, d), row), pl.BlockSpec((tm, d), row), pl.BlockSpec((tm, w_b), row),
                  pl.BlockSpec((tm, w_b), row), pl.BlockSpec((tm, N_BRANCH * d), row),
                  pl.BlockSpec((3, None, BS, d), lambda i: (0, mod_idx(i), 0, 0)),
                  pl.BlockSpec((d, d), const), pl.BlockSpec((w_b, d), const), pl.BlockSpec((w_b, d), const),
                  pl.BlockSpec((d, d), const), pl.BlockSpec((1, d), const), pl.BlockSpec((1, d), const),
                  pl.BlockSpec((d, ROUTER_PAD), const), pl.BlockSpec((1, ROUTER_PAD), const)],
        out_specs=[pl.BlockSpec((tm, d), row), pl.BlockSpec((tm, d), row), pl.BlockSpec((tm, ROUTER_PAD), row)],
        out_shape=[jax.ShapeDtypeStruct((n, d), F32), jax.ShapeDtypeStruct((n, d), BF16),
                   jax.ShapeDtypeStruct((n, ROUTER_PAD), F32)],
        compiler_params=_cparams(("arbitrary",)),
        name="merge_out_projection",
    )(x, la, pb, sg, gt, modt, oa, ob, oc, wo, bo, g2, rw, rb)


def _route_kernel(lg_ref, e_ref, p_ref, r_ref, cnt_ref, carry):
    @pl.when(pl.program_id(0) == 0)
    def _():
        carry[...] = jnp.zeros_like(carry)

    lg = lg_ref[...]
    tm = lg.shape[0]
    lane = lax.broadcasted_iota(jnp.int32, (tm, ROUTER_PAD), 1).astype(F32)
    work = lg
    sel = jnp.zeros((tm, ROUTER_PAD), F32)
    vals, idxs = [], []
    for _ in range(TOP_K):
        m = jnp.max(work, axis=-1, keepdims=True)
        idx = jnp.min(jnp.where(work == m, lane, float(ROUTER_PAD)), axis=-1, keepdims=True)
        hit = lane == idx
        sel = jnp.where(hit, 1.0, sel)
        work = jnp.where(hit, -jnp.inf, work)
        vals.append(m)
        idxs.append(idx)
    exps = [jnp.exp(v - vals[0]) for v in vals]
    denom = exps[0]
    for ex in exps[1:]:
        denom = denom + ex

    r_i = lax.broadcasted_iota(jnp.int32, (tm, tm), 0)
    c_i = lax.broadcasted_iota(jnp.int32, (tm, tm), 1)
    ltri = jnp.where(r_i > c_i, 1.0, 0.0).astype(BF16)
    before = jnp.dot(ltri, sel.astype(BF16), preferred_element_type=F32) + carry[...]

    e_out = jnp.zeros((tm, ROUTER_PAD), F32)
    p_out = jnp.zeros((tm, ROUTER_PAD), F32)
    r_out = jnp.zeros((tm, ROUTER_PAD), F32)
    for k in range(TOP_K):
        rank = jnp.sum(jnp.where(lane == idxs[k], before, 0.0), axis=-1, keepdims=True)
        slot = lane == float(k)
        e_out = jnp.where(slot, idxs[k], e_out)
        p_out = jnp.where(slot, exps[k] / denom, p_out)
        r_out = jnp.where(slot, rank, r_out)
    e_ref[...] = e_out.astype(jnp.int32)
    p_ref[...] = p_out
    r_ref[...] = r_out.astype(jnp.int32)
    carry[...] = carry[...] + jnp.sum(sel, axis=0, keepdims=True)
    cnt_ref[...] = carry[...].astype(jnp.int32)


def _route(logits):
    n = logits.shape[0]
    tm = ROW_TILE
    row = lambda i: (i, 0)
    return pl.pallas_call(
        _route_kernel,
        grid=(n // tm,),
        in_specs=[pl.BlockSpec((tm, ROUTER_PAD), row)],
        out_specs=[pl.BlockSpec((tm, ROUTER_PAD), row), pl.BlockSpec((tm, ROUTER_PAD), row),
                   pl.BlockSpec((tm, ROUTER_PAD), row), pl.BlockSpec((1, ROUTER_PAD), lambda i: (0, 0))],
        out_shape=[jax.ShapeDtypeStruct((n, ROUTER_PAD), jnp.int32), jax.ShapeDtypeStruct((n, ROUTER_PAD), F32),
                   jax.ShapeDtypeStruct((n, ROUTER_PAD), jnp.int32), jax.ShapeDtypeStruct((1, ROUTER_PAD), jnp.int32)],
        scratch_shapes=[pltpu.VMEM((1, ROUTER_PAD), F32)],
        compiler_params=_cparams(("arbitrary",)),
        name="moe_route",
    )(logits)


def _work_list(counts, n_assign):
    tm = MOE_TILE
    n_experts = counts.shape[0]
    n_items = n_assign // tm + n_experts - 1
    end = jnp.cumsum(counts)
    start = end - counts
    first_blk = start // tm
    n_it = jnp.where(counts > 0, (end - 1) // tm - first_blk + 1, 0)
    it_end = jnp.cumsum(n_it)
    it_start = it_end - n_it
    w = jnp.arange(n_items, dtype=jnp.int32)
    wv = jnp.minimum(w, it_end[-1] - 1)
    e_w = jnp.sum((it_end[None, :] <= wv[:, None]).astype(jnp.int32), axis=1)
    oh = (e_w[:, None] == jnp.arange(n_experts, dtype=jnp.int32)[None, :]).astype(jnp.int32)
    pick = lambda v: jnp.sum(oh * v[None, :], axis=1)
    b_w = pick(first_blk) + wv - pick(it_start)
    lo = jnp.maximum(pick(start), b_w * tm) - b_w * tm
    hi = jnp.minimum(pick(end), (b_w + 1) * tm) - b_w * tm
    valid = w < it_end[-1]
    lo = jnp.where(valid, lo, 0)
    hi = jnp.where(valid, hi, 0)
    first = jnp.logical_and(valid, jnp.logical_or(w == 0, b_w != jnp.roll(b_w, 1)))
    i32 = lambda v: v.astype(jnp.int32)
    return i32(b_w), i32(e_w), i32(lo), i32(hi), i32(first), start


def _expert_kernel(wb, we, wlo, whi, wfirst, x_ref, wg_ref, bg_ref, wu_ref, bu_ref, wd_ref, bd_ref, o_ref):
    del wb, we
    w = pl.program_id(0)
    lo, hi, first = wlo[w], whi[w], wfirst[w]

    @pl.when(hi > lo)
    def _():
        xb = x_ref[...]
        gate = jnp.minimum(jnp.dot(xb, wg_ref[...], preferred_element_type=F32) + bg_ref[...], SWIGLU_LIMIT)
        up = jnp.clip(jnp.dot(xb, wu_ref[...], preferred_element_type=F32) + bu_ref[...],
                      -SWIGLU_LIMIT, SWIGLU_LIMIT)
        glu = gate * _sigmoid(gate * SWIGLU_ALPHA)
        act = ((up + 1.0) * glu).astype(BF16)
        res = (jnp.dot(act, wd_ref[...], preferred_element_type=F32) + bd_ref[...]).astype(BF16)
        row = lax.broadcasted_iota(jnp.int32, res.shape, 0)
        mine = jnp.logical_and(row >= lo, row < hi)

        @pl.when(first == 1)
        def _():
            o_ref[...] = jnp.where(mine, res, jnp.zeros_like(res))

        @pl.when(first == 0)
        def _():
            o_ref[...] = jnp.where(mine, res, o_ref[...])


def _experts(work, xg, wg, bg, wu, bu, wd, bd):
    n_rows, d = xg.shape
    dff = wg.shape[2]
    tm = MOE_TILE
    wspec = lambda r, c: pl.BlockSpec((None, r, c), lambda i, wb, we, *_: (we[i], 0, 0))
    xspec = pl.BlockSpec((tm, d), lambda i, wb, *_: (wb[i], 0))
    return pl.pallas_call(
        _expert_kernel,
        grid_spec=pltpu.PrefetchScalarGridSpec(
            num_scalar_prefetch=5,
            grid=(work[0].shape[0],),
            in_specs=[xspec, wspec(d, dff), wspec(1, dff), wspec(d, dff), wspec(1, dff), wspec(dff, d), wspec(1, d)],
            out_specs=xspec),
        out_shape=jax.ShapeDtypeStruct((n_rows, d), BF16),
        compiler_params=_cparams(("arbitrary",)),
        name="moe_experts",
    )(*work, xg, wg, bg, wu, bu, wd, bd)


def _combine_kernel(x_ref, y_ref, p_ref, g_ref, o_ref):
    rows, d = x_ref.shape
    f = p_ref[:, 0:1] * y_ref[0].astype(F32)
    for k in range(1, TOP_K):
        f = f + p_ref[:, k:k + 1] * y_ref[k].astype(F32)
    o_ref[...] = x_ref[...] + (f.reshape(rows // BS, BS, d) * g_ref[...][None]).reshape(rows, d)


def _combine(x_mid, ypg, probs, g2t, mod_idx):
    n = ypg.shape[1]
    d = x_mid.shape[1]
    tm = ROW_TILE
    row = lambda i: (i, 0)
    return pl.pallas_call(
        _combine_kernel,
        grid=(n // tm,),
        in_specs=[pl.BlockSpec((tm, d), row), pl.BlockSpec((TOP_K, tm, d), lambda i: (0, i, 0)),
                  pl.BlockSpec((tm, ROUTER_PAD), row),
                  pl.BlockSpec((None, BS, d), lambda i: (mod_idx(i), 0, 0))],
        out_specs=pl.BlockSpec((tm, d), row),
        out_shape=jax.ShapeDtypeStruct((n, d), F32),
        compiler_params=_cparams(("arbitrary",)),
        name="moe_combine",
    )(x_mid, ypg, probs, g2t)


def _final_kernel(x_ref, g_ref, o_ref):
    o_ref[...] = _rms(x_ref[...], g_ref[...])


def _final_norm(x, g, n_rows):
    d = x.shape[1]
    tm = ROW_TILE
    return pl.pallas_call(
        _final_kernel,
        grid=(n_rows // tm,),
        in_specs=[pl.BlockSpec((tm, d), lambda i: (i, 0)), pl.BlockSpec((1, d), lambda i: (0, 0))],
        out_specs=pl.BlockSpec((tm, d), lambda i: (i, 0)),
        out_shape=jax.ShapeDtypeStruct((n_rows, d), F32),
        compiler_params=_cparams(("arbitrary",)),
        name="final_rmsnorm",
    )(x, g)


def _to_rows(a):
    b, t, d = a.shape
    return a.reshape(b // BS, BS, t, d).transpose(0, 2, 1, 3).reshape(b * t, d)


def _from_rows(r, b, t):
    d = r.shape[1]
    return r.reshape(b // BS, t, BS, d).transpose(0, 2, 1, 3).reshape(b, t, d)


def kernel(x, c, ctx, c_ctx, ada_w, ada_b, norm1_g, norm2_g, w_in, b_in, conv_w, conv_b, lru_wa, lru_ba, lru_wx, lru_bx, lru_lambda, out_a, pool_w, pool_b, pool_scale, out_b, sg_ln_g, sg_ln_b, sg_w, sg_b, out_c, w_o, b_o, router_w, router_b, w_gate, b_gate, w_up, b_up, w_down, b_down, final_g):
    bsz, seq, d = x.shape
    n_ctx = ctx.shape[1]
    depth = ada_w.shape[0]
    n_experts = router_w.shape[2]
    nbg = bsz // BS
    rl, rc = seq * BS, n_ctx * BS
    geom = (nbg, rl, rc)
    n_lat = nbg * rl
    assert bsz % BS == 0 and seq % GRID_W == 0 and rc % ROW_TILE == 0 and rl % rc == 0
    assert rc % LRU_ROWS == 0 and rc % (CHUNK * BS) == 0 and d // H_A == LANES

    tiles_lat_bg = rl // ROW_TILE
    n_lat_tiles = nbg * tiles_lat_bg
    mod_idx = lambda i: jnp.where(i < n_lat_tiles, i // tiles_lat_bg, nbg)

    xs = jnp.concatenate([_to_rows(x), _to_rows(ctx)], axis=0)

    cc = jnp.concatenate([c, c_ctx[None], jnp.zeros((BS - 1, d), F32)], axis=0)
    mod = _modulation(cc, ada_w, ada_b)
    mod = mod.reshape(depth, bsz + BS, 6, d).transpose(0, 2, 1, 3)
    mod_lat = mod[:, :, :bsz].reshape(depth, 6, nbg, BS, d)
    mod_ctx = jnp.broadcast_to(mod[:, :, bsz:bsz + 1, None, :], (depth, 6, 1, BS, d))
    modt = jnp.concatenate([mod_lat, mod_ctx], axis=2)

    eye = jnp.eye(BS, dtype=F32)
    for l in range(depth):
        m = modt[l]
        w_in_b = w_in[l].astype(BF16)
        xa, ga, xb, uv, gt = _inproj(xs, m[0:2], norm1_g[l][None], w_in_b, b_in[l][None], mod_idx)

        hsf = None
        for dr in range(2):
            wgate = jnp.concatenate([lru_wa[l, dr], lru_wx[l, dr]], axis=-1).astype(BF16)
            dh = d // H_A
            bgate = jnp.concatenate([lru_ba[l, dr].reshape(H_A, 1, dh), lru_bx[l, dr].reshape(H_A, 1, dh)], axis=-1)
            args = (xa, conv_w[l], conv_b[l][None], wgate, bgate, lru_lambda[l, dr][None], geom)
            if dr == 0:
                hsf = _lru(*args, reverse=False)
            else:
                la = _lru(*args, reverse=True, hsf=hsf, ga=ga)

        pw = pool_w[l].astype(BF16)
        pbm = _pool(xb, pw, pool_b[l][None], pool_scale[l][None], geom)
        pbm = _pool(xb, pw, pool_b[l][None], pool_scale[l][None], geom, prev=pbm)

        wk = jnp.einsum('gpq,bc->gpbqc', sg_w[l], eye).reshape(G_C, CHUNK * BS, CHUNK * BS).astype(BF16)
        w_c = d // 2
        sbias = jnp.broadcast_to(sg_b[l].T[:, None, :, None], (CHUNK, BS, G_C, w_c // G_C)).reshape(CHUNK * BS, w_c)
        sgo = _sgu(uv, sg_ln_g[l][None], sg_ln_b[l][None], wk, sbias)

        rw = jnp.pad(router_w[l], ((0, 0), (0, ROUTER_PAD - n_experts)))
        rb = jnp.pad(router_b[l], (0, ROUTER_PAD - n_experts), constant_values=ROUTER_PAD_LOGIT)[None]
        x_mid, h2, logits = _merge(xs, la, pbm, sgo, gt, m[2:5], out_a[l].astype(BF16), out_b[l].astype(BF16),
                                   out_c[l].astype(BF16), w_o[l].astype(BF16), b_o[l][None], norm2_g[l][None],
                                   rw, rb, mod_idx)

        last = l == depth - 1
        n_tok = n_lat if last else xs.shape[0]
        n_assign = n_tok * TOP_K
        top_e, probs, rank, counts = _route(logits[:n_tok])
        *work, start = _work_list(counts[0, :n_experts], n_assign)
        top_e, rank = top_e[:, :TOP_K], rank[:, :TOP_K]
        e_hot = top_e[:, :, None] == jnp.arange(n_experts, dtype=jnp.int32)[None, None, :]
        pos = rank + jnp.sum(jnp.where(e_hot, start[None, None, :], 0), axis=-1)
        tok_ids = jnp.arange(n_assign, dtype=jnp.int32) // TOP_K
        _, sorted_tok = lax.sort((pos.reshape(-1), tok_ids), num_keys=1)
        xg = h2.at[sorted_tok].get(mode='promise_in_bounds')
        yp = _experts(work, xg, w_gate[l].astype(BF16), b_gate[l][:, None], w_up[l].astype(BF16),
                      b_up[l][:, None], w_down[l].astype(BF16), b_down[l][:, None])
        ypg = yp.at[pos.T].get(mode='promise_in_bounds')
        xs = _combine(x_mid, ypg, probs, m[5], mod_idx)

    out = _final_norm(xs, final_g[None], n_lat)
    return _from_rows(out, bsz, seq)
```

```python
import functools

import jax
import jax.numpy as jnp
from jax import lax
from jax.experimental import pallas as pl
from jax.experimental.pallas import tpu as pltpu

F32 = jnp.float32
BF16 = jnp.bfloat16

GRID_W = 64
EPS = 1e-6
H_A = 8
CONV_W = 4
LRU_C = 8.0
POOL_WINDOWS = (2, 4, 8, 16)
CHUNK = 128
G_C = 4
N_BRANCH = 3
TOP_K = 4
SWIGLU_LIMIT = 7.0
SWIGLU_ALPHA = 1.702

LANES = 128
BS = 8
VMEM_LIMIT = 60 * 1024 * 1024

ROW_TILE = 512
LRU_ROWS = 512
MOE_TILE = 512
ROUTER_PAD = 128
ROUTER_PAD_LOGIT = -1e30


def _cparams(sem):
    return pltpu.CompilerParams(dimension_semantics=sem, vmem_limit_bytes=VMEM_LIMIT)


def _sigmoid(v):
    return 0.5 * jnp.tanh(0.5 * v) + 0.5


def _rms(x, g):
    return x * lax.rsqrt(jnp.mean(x * x, axis=-1, keepdims=True) + EPS) * g


def _modulate(y, shift, scale):
    rows, d = y.shape
    y3 = y.reshape(rows // BS, BS, d)
    return (y3 * (1.0 + scale)[None] + shift[None]).reshape(rows, d)


def _mod_kernel(c_ref, w_ref, b_ref, o_ref):
    s = c_ref[...]
    s = s * _sigmoid(s)
    o_ref[...] = jnp.dot(s, w_ref[...], preferred_element_type=F32,
                         precision=lax.Precision.HIGHEST) + b_ref[...]


def _modulation(cc, ada_w, ada_b):
    depth, d, n6 = ada_w.shape
    r = cc.shape[0]
    tn = 1536
    return pl.pallas_call(
        _mod_kernel,
        grid=(depth, n6 // tn),
        in_specs=[pl.BlockSpec((r, d), lambda l, j: (0, 0)),
                  pl.BlockSpec((None, d, tn), lambda l, j: (l, 0, j)),
                  pl.BlockSpec((None, 1, tn), lambda l, j: (l, 0, j))],
        out_specs=pl.BlockSpec((None, r, tn), lambda l, j: (l, 0, j)),
        out_shape=jax.ShapeDtypeStruct((depth, r, n6), F32),
        compiler_params=_cparams(("arbitrary", "arbitrary")),
        name="adaln_modulation",
    )(cc, ada_w, ada_b.reshape(depth, 1, n6))


def _inproj_kernel(x_ref, mod_ref, g_ref, w_ref, b_ref, xa_ref, ga_ref, xb_ref, uv_ref, gt_ref):
    d = x_ref.shape[1]
    w_b = xb_ref.shape[1]
    h = _modulate(_rms(x_ref[...], g_ref[...]), mod_ref[0], mod_ref[1]).astype(BF16)

    def proj(lo, n):
        return jnp.dot(h, w_ref[:, lo:lo + n], preferred_element_type=F32) + b_ref[:, lo:lo + n]

    xa_ref[...] = proj(0, d)
    ga_ref[...] = jax.nn.gelu(proj(d, d)).astype(BF16)
    xb_ref[...] = proj(2 * d, w_b)
    off_u = 2 * d + w_b
    uv_ref[...] = jax.nn.gelu(proj(off_u, d)).astype(BF16)
    off_g = off_u + d
    for j in range(N_BRANCH):
        gt_ref[:, j * d:(j + 1) * d] = _sigmoid(proj(off_g + j * d, d)).astype(BF16)


def _inproj(x, modt, g, w, b, mod_idx):
    n, d = x.shape
    cols = w.shape[1]
    w_b = d // 2
    tm = ROW_TILE
    row = lambda i: (i, 0)
    const = lambda i: (0, 0)
    return pl.pallas_call(
        _inproj_kernel,
        grid=(n // tm,),
        in_specs=[pl.BlockSpec((tm, d), row),
                  pl.BlockSpec((2, None, BS, d), lambda i: (0, mod_idx(i), 0, 0)),
                  pl.BlockSpec((1, d), const),
                  pl.BlockSpec((d, cols), const),
                  pl.BlockSpec((1, cols), const)],
        out_specs=[pl.BlockSpec((tm, d), row), pl.BlockSpec((tm, d), row),
                   pl.BlockSpec((tm, w_b), row), pl.BlockSpec((tm, d), row),
                   pl.BlockSpec((tm, N_BRANCH * d), row)],
        out_shape=[jax.ShapeDtypeStruct((n, d), F32), jax.ShapeDtypeStruct((n, d), BF16),
                   jax.ShapeDtypeStruct((n, w_b), F32), jax.ShapeDtypeStruct((n, d), BF16),
                   jax.ShapeDtypeStruct((n, N_BRANCH * d), BF16)],
        compiler_params=_cparams(("arbitrary",)),
        name="in_projection",
    )(x, modt, g, w, b)


def _lru_kernel(*refs, reverse, nc_ctx, nc_lat):
    if reverse:
        (xa_ref, xp_ref, xn_ref, cw_ref, cb_ref, wg_ref, bg_ref, lam_ref, hsf_ref, ga_ref,
         o_ref, a_s, b_s, h_s) = refs
    else:
        (xa_ref, xp_ref, xn_ref, cw_ref, cb_ref, wg_ref, bg_ref, lam_ref,
         o_ref, a_s, b_s, h_s) = refs
    rows, d = xa_ref.shape
    dh = d // H_A
    nt = rows // BS
    i = pl.program_id(1)
    is_ctx = i < nc_ctx
    if reverse:
        j = jnp.where(is_ctx, nc_ctx - 1 - i, nc_ctx + nc_lat - 1 - i)
    else:
        j = jnp.where(is_ctx, i, i - nc_ctx)
    n_seq = jnp.where(is_ctx, nc_ctx, nc_lat)

    @pl.when(i == 0)
    def _():
        h_s[...] = jnp.zeros_like(h_s)

    keep_prev = j > 0
    keep_next = j < n_seq - 1

    lam = lam_ref[...]
    softplus_neg = jnp.maximum(-lam, 0.0) + jnp.log(1.0 + jnp.exp(-jnp.abs(lam)))
    c1 = (-0.5 * LRU_C) * softplus_neg

    for hd in range(H_A):
        cs = slice(hd * dh, (hd + 1) * dh)
        x0 = xa_ref[:, cs]
        ext = jnp.concatenate([jnp.where(keep_prev, xp_ref[:, cs], 0.0), x0,
                               jnp.where(keep_next, xn_ref[:, cs], 0.0)], axis=0)
        xc = cb_ref[:, cs] + cw_ref[0:1, cs] * ext[0:rows]
        for tap in range(1, CONV_W):
            xc = xc + cw_ref[tap:tap + 1, cs] * ext[tap * BS:tap * BS + rows]
        th = jnp.tanh(jnp.dot(xc.astype(BF16), wg_ref[hd], preferred_element_type=F32) + bg_ref[hd])
        log_a = th[:, :dh] * c1[:, cs] + c1[:, cs]
        a_s[:, cs] = jnp.exp(log_a)
        t = jnp.tanh(log_a)
        q = (-0.5 * t) / (1.0 - t)
        u = jnp.where(q > 0.0, q * lax.rsqrt(q), 0.0) * xc
        b_s[:, cs] = u * th[:, dh:] + u

    hs_ref = b_s if reverse else o_ref

    def step(s, h):
        t = (nt - 1 - s) if reverse else s
        r0 = pl.multiple_of(t * BS, BS)
        h = a_s[pl.ds(r0, BS), :] * h + b_s[pl.ds(r0, BS), :]
        hs_ref[pl.ds(r0, BS), :] = h
        return h

    h_s[...] = lax.fori_loop(0, nt, step, h_s[...], unroll=8)

    if reverse:
        o_ref[...] = (ga_ref[...].astype(F32) * (hsf_ref[...] + b_s[...])).astype(BF16)


def _lru(xa, cw, cb, wg, bg, lam, geom, reverse, hsf=None, ga=None):
    n, d = xa.shape
    nbg, rl, rc = geom
    cr = LRU_ROWS
    nc_ctx, nc_lat = rc // cr, rl // cr
    pad_lo = (CONV_W // 2) * BS
    pad_hi = (CONV_W - 1 - CONV_W // 2) * BS

    def blk(bg_i, i):
        is_ctx = i < nc_ctx
        if reverse:
            j = jnp.where(is_ctx, nc_ctx - 1 - i, nc_ctx + nc_lat - 1 - i)
        else:
            j = jnp.where(is_ctx, i, i - nc_ctx)
        return jnp.where(is_ctx, (nbg * rl + bg_i * rc) // cr + j, bg_i * nc_lat + j)

    cur = lambda b, i: (blk(b, i), 0)
    prv = lambda b, i: (jnp.maximum(blk(b, i) * (cr // pad_lo) - 1, 0), 0)
    nxt = lambda b, i: (jnp.minimum((blk(b, i) + 1) * (cr // pad_hi), n // pad_hi - 1), 0)
    const2 = lambda b, i: (0, 0)
    const3 = lambda b, i: (0, 0, 0)
    in_specs = [pl.BlockSpec((cr, d), cur), pl.BlockSpec((pad_lo, d), prv), pl.BlockSpec((pad_hi, d), nxt),
                pl.BlockSpec((CONV_W, d), const2), pl.BlockSpec((1, d), const2),
                pl.BlockSpec((H_A, d // H_A, 2 * d // H_A), const3),
                pl.BlockSpec((H_A, 1, 2 * d // H_A), const3),
                pl.BlockSpec((1, d), const2)]
    args = [xa, xa, xa, cw, cb, wg, bg, lam]
    if reverse:
        in_specs += [pl.BlockSpec((cr, d), cur), pl.BlockSpec((cr, d), cur)]
        args += [hsf, ga]
    return pl.pallas_call(
        functools.partial(_lru_kernel, reverse=reverse, nc_ctx=nc_ctx, nc_lat=nc_lat),
        grid=(nbg, nc_ctx + nc_lat),
        in_specs=in_specs,
        out_specs=pl.BlockSpec((cr, d), cur),
        out_shape=jax.ShapeDtypeStruct((n, d), BF16 if reverse else F32),
        scratch_shapes=[pltpu.VMEM((cr, d), F32), pltpu.VMEM((cr, d), F32), pltpu.VMEM((BS, d), F32)],
        compiler_params=_cparams(("arbitrary", "arbitrary")),
        name="rglru_reverse" if reverse else "rglru_forward",
    )(*args)


def _window_count(pos, k, n):
    return jnp.minimum(pos + (k - k // 2), n) - jnp.maximum(pos - k // 2, 0)


def _pool_kernel(x_ref, w_ref, b_ref, sc_ref, o_ref, pad_s, *, n_lines, npix):
    g = pl.program_id(1)
    n = npix * BS

    def variant(k):
        lo = (k // 2) * BS
        hi = (k - k // 2 - 1) * BS
        pad_s[0:lo, :] = jnp.zeros((lo, LANES), F32)
        if hi:
            pad_s[lo + n:lo + n + hi, :] = jnp.zeros((hi, LANES), F32)
        pix = lax.broadcasted_iota(jnp.int32, (n, LANES), 0) // BS
        cnt_pix = _window_count(pix, k, npix).astype(F32)

        def line_body(r, carry):
            base = pl.multiple_of(r * n, n)
            seg = x_ref[pl.ds(base, n), :]
            if n_lines > 1:
                acc = jnp.zeros((n, LANES), F32)
                for off in range(-(k // 2), k - k // 2):
                    rr = r + off
                    ok = jnp.logical_and(rr >= 0, rr < n_lines)
                    rc = jnp.clip(rr, 0, n_lines - 1)
                    src = x_ref[pl.ds(pl.multiple_of(rc * n, n), n), :]
                    acc = acc + jnp.where(ok, src, 0.0)
                line = acc / _window_count(r, k, n_lines).astype(F32)
            else:
                line = seg
            pad_s[lo:lo + n, :] = line
            acc2 = pad_s[0:n, :]
            for jj in range(1, k):
                acc2 = acc2 + pad_s[jj * BS:jj * BS + n, :]
            p = acc2 / cnt_pix - seg
            y = jnp.dot(p.astype(BF16), w_ref[...], preferred_element_type=F32) + b_ref[...]
            o_ref[pl.ds(base, n), :] = (y * sc_ref[...]).astype(BF16)
            return carry

        lax.fori_loop(0, n_lines, line_body, 0)

    for gi, k in enumerate(POOL_WINDOWS):
        pl.when(g == gi)(functools.partial(variant, k))


def _pool(xb, pw, pb, ps, geom, context):
    nbg, rl, rc = geom
    ng = len(POOL_WINDOWS)
    if context:
        rows, n_lines, npix, base = rc, 1, rc // BS, (nbg * rl) // rc
    else:
        rows, n_lines, npix, base = rl, rl // (GRID_W * BS), GRID_W, 0
    kmax = max(POOL_WINDOWS)
    gconst = lambda b, g: (0, g)
    return pl.pallas_call(
        functools.partial(_pool_kernel, n_lines=n_lines, npix=npix),
        grid=(nbg, ng),
        in_specs=[pl.BlockSpec((rows, LANES), lambda b, g: (base + b, g)),
                  pl.BlockSpec((None, LANES, LANES), lambda b, g: (g, 0, 0)),
                  pl.BlockSpec((1, LANES), gconst), pl.BlockSpec((1, LANES), gconst)],
        out_specs=pl.BlockSpec((rows, LANES), lambda b, g: (b, g)),
        out_shape=jax.ShapeDtypeStruct((nbg * rows, ng * LANES), BF16),
        scratch_shapes=[pltpu.VMEM(((npix + kmax) * BS, LANES), F32)],
        compiler_params=_cparams(("arbitrary", "arbitrary")),
        name="pool_context" if context else "pool_latent",
    )(xb, pw, pb, ps)


def _sgu_kernel(uv_ref, lng_ref, lnb_ref, wk_ref, bias_ref, o_ref):
    w_c = o_ref.shape[1]
    dg = w_c // G_C
    v = uv_ref[:, w_c:].astype(F32)
    mu = jnp.mean(v, axis=-1, keepdims=True)
    vc = v - mu
    var = jnp.mean(vc * vc, axis=-1, keepdims=True)
    vn = (vc * lax.rsqrt(var + EPS) * lng_ref[...] + lnb_ref[...]).astype(BF16)
    for g in range(G_C):
        cs = slice(g * dg, (g + 1) * dg)
        s = jnp.dot(wk_ref[g], vn[:, cs], preferred_element_type=F32) + bias_ref[:, cs]
        o_ref[:, cs] = (uv_ref[:, cs].astype(F32) * s).astype(BF16)


def _sgu(uv, lng, lnb, wk, bias):
    n, d = uv.shape
    w_c = d // 2
    rows = CHUNK * BS
    return pl.pallas_call(
        _sgu_kernel,
        grid=(n // rows,),
        in_specs=[pl.BlockSpec((rows, d), lambda i: (i, 0)),
                  pl.BlockSpec((1, w_c), lambda i: (0, 0)), pl.BlockSpec((1, w_c), lambda i: (0, 0)),
                  pl.BlockSpec((G_C, rows, rows), lambda i: (0, 0, 0)),
                  pl.BlockSpec((rows, w_c), lambda i: (0, 0))],
        out_specs=pl.BlockSpec((rows, w_c), lambda i: (i, 0)),
        out_shape=jax.ShapeDtypeStruct((n, w_c), BF16),
        compiler_params=_cparams(("arbitrary",)),
        name="spatial_gate",
    )(uv, lng, lnb, wk, bias)


def _merge_kernel(x_ref, la_ref, pbl_ref, pbc_ref, sg_ref, gt_ref, mod_ref, oa_ref, ob_ref, oc_ref, wo_ref, bo_ref,
                  g2_ref, rw_ref, rb_ref, xo_ref, h2_ref, lg_ref, *, n_lat_tiles):
    d = x_ref.shape[1]
    pooled = jnp.where(pl.program_id(0) < n_lat_tiles, pbl_ref[...], pbc_ref[...])
    y_a = jnp.dot(la_ref[...], oa_ref[...], preferred_element_type=F32)
    y_b = jnp.dot(pooled, ob_ref[...], preferred_element_type=F32)
    y_c = jnp.dot(sg_ref[...], oc_ref[...], preferred_element_type=F32)
    merged = (gt_ref[:, 0:d].astype(F32) * y_a + gt_ref[:, d:2 * d].astype(F32) * y_b
              + gt_ref[:, 2 * d:3 * d].astype(F32) * y_c)
    y = jnp.dot(merged.astype(BF16), wo_ref[...], preferred_element_type=F32) + bo_ref[...]
    rows = y.shape[0]
    gate1 = mod_ref[0]
    xn = x_ref[...] + (y.reshape(rows // BS, BS, d) * gate1[None]).reshape(rows, d)
    xo_ref[...] = xn
    h2 = _modulate(_rms(xn, g2_ref[...]), mod_ref[1], mod_ref[2])
    h2_ref[...] = h2.astype(BF16)
    h_hi = h2.astype(BF16)
    h_lo = (h2 - h_hi.astype(F32)).astype(BF16)
    rw = rw_ref[...]
    w_hi = rw.astype(BF16)
    w_lo = (rw - w_hi.astype(F32)).astype(BF16)
    lg = jnp.dot(h_hi, w_hi, preferred_element_type=F32)
    lg = lg + jnp.dot(h_lo, w_hi, preferred_element_type=F32)
    lg = lg + jnp.dot(h_hi, w_lo, preferred_element_type=F32)
    lg_ref[...] = lg + rb_ref[...]


def _merge(x, la, pbl, pbc, sg, gt, modt, oa, ob, oc, wo, bo, g2, rw, rb, mod_idx):
    n, d = x.shape
    w_b = pbl.shape[1]
    n_lat_tiles = pbl.shape[0] // ROW_TILE
    tm = ROW_TILE
    row = lambda i: (i, 0)
    const = lambda i: (0, 0)
    return pl.pallas_call(
        functools.partial(_merge_kernel, n_lat_tiles=n_lat_tiles),
        grid=(n // tm,),
        in_specs=[pl.BlockSpec((tm, d), row), pl.BlockSpec((tm, d), row),
                  pl.BlockSpec((tm, w_b), lambda i: (jnp.minimum(i, n_lat_tiles - 1), 0)),
                  pl.BlockSpec((tm, w_b), lambda i: (jnp.maximum(i - n_lat_tiles, 0), 0)),
                  pl.BlockSpec((tm, w_b), row), pl.BlockSpec((tm, N_BRANCH * d), row),
                  pl.BlockSpec((3, None, BS, d), lambda i: (0, mod_idx(i), 0, 0)),
                  pl.BlockSpec((d, d), const), pl.BlockSpec((w_b, d), const), pl.BlockSpec((w_b, d), const),
                  pl.BlockSpec((d, d), const), pl.BlockSpec((1, d), const), pl.BlockSpec((1, d), const),
                  pl.BlockSpec((d, ROUTER_PAD), const), pl.BlockSpec((1, ROUTER_PAD), const)],
        out_specs=[pl.BlockSpec((tm, d), row), pl.BlockSpec((tm, d), row), pl.BlockSpec((tm, ROUTER_PAD), row)],
        out_shape=[jax.ShapeDtypeStruct((n, d), F32), jax.ShapeDtypeStruct((n, d), BF16),
                   jax.ShapeDtypeStruct((n, ROUTER_PAD), F32)],
        compiler_params=_cparams(("arbitrary",)),
        name="merge_out_projection",
    )(x, la, pbl, pbc, sg, gt, modt, oa, ob, oc, wo, bo, g2, rw, rb)


def _route_kernel(lg_ref, e_ref, p_ref, r_ref, cnt_ref, carry):
    @pl.when(pl.program_id(0) == 0)
    def _():
        carry[...] = jnp.zeros_like(carry)

    lg = lg_ref[...]
    tm = lg.shape[0]
    lane = lax.broadcasted_iota(jnp.int32, (tm, ROUTER_PAD), 1).astype(F32)
    work = lg
    sel = jnp.zeros((tm, ROUTER_PAD), F32)
    vals, idxs = [], []
    for _ in range(TOP_K):
        m = jnp.max(work, axis=-1, keepdims=True)
        idx = jnp.min(jnp.where(work == m, lane, float(ROUTER_PAD)), axis=-1, keepdims=True)
        hit = lane == idx
        sel = jnp.where(hit, 1.0, sel)
        work = jnp.where(hit, -jnp.inf, work)
        vals.append(m)
        idxs.append(idx)
    exps = [jnp.exp(v - vals[0]) for v in vals]
    denom = exps[0]
    for ex in exps[1:]:
        denom = denom + ex

    r_i = lax.broadcasted_iota(jnp.int32, (tm, tm), 0)
    c_i = lax.broadcasted_iota(jnp.int32, (tm, tm), 1)
    ltri = jnp.where(r_i > c_i, 1.0, 0.0).astype(BF16)
    before = jnp.dot(ltri, sel.astype(BF16), preferred_element_type=F32) + carry[...]

    e_out = jnp.zeros((tm, ROUTER_PAD), F32)
    p_out = jnp.zeros((tm, ROUTER_PAD), F32)
    r_out = jnp.zeros((tm, ROUTER_PAD), F32)
    for k in range(TOP_K):
        rank = jnp.sum(jnp.where(lane == idxs[k], before, 0.0), axis=-1, keepdims=True)
        slot = lane == float(k)
        e_out = jnp.where(slot, idxs[k], e_out)
        p_out = jnp.where(slot, exps[k] / denom, p_out)
        r_out = jnp.where(slot, rank, r_out)
    e_ref[...] = e_out.astype(jnp.int32)
    p_ref[...] = p_out
    r_ref[...] = r_out.astype(jnp.int32)
    carry[...] = carry[...] + jnp.sum(sel, axis=0, keepdims=True)
    cnt_ref[...] = carry[...].astype(jnp.int32)


def _route(logits):
    n = logits.shape[0]
    tm = ROW_TILE
    row = lambda i: (i, 0)
    return pl.pallas_call(
        _route_kernel,
        grid=(n // tm,),
        in_specs=[pl.BlockSpec((tm, ROUTER_PAD), row)],
        out_specs=[pl.BlockSpec((tm, ROUTER_PAD), row), pl.BlockSpec((tm, ROUTER_PAD), row),
                   pl.BlockSpec((tm, ROUTER_PAD), row), pl.BlockSpec((1, ROUTER_PAD), lambda i: (0, 0))],
        out_shape=[jax.ShapeDtypeStruct((n, ROUTER_PAD), jnp.int32), jax.ShapeDtypeStruct((n, ROUTER_PAD), F32),
                   jax.ShapeDtypeStruct((n, ROUTER_PAD), jnp.int32), jax.ShapeDtypeStruct((1, ROUTER_PAD), jnp.int32)],
        scratch_shapes=[pltpu.VMEM((1, ROUTER_PAD), F32)],
        compiler_params=_cparams(("arbitrary",)),
        name="moe_route",
    )(logits)


def _work_list(counts, n_assign):
    tm = MOE_TILE
    n_experts = counts.shape[0]
    n_items = n_assign // tm + n_experts - 1
    end = jnp.cumsum(counts)
    start = end - counts
    first_blk = start // tm
    n_it = jnp.where(counts > 0, (end - 1) // tm - first_blk + 1, 0)
    it_end = jnp.cumsum(n_it)
    it_start = it_end - n_it
    w = jnp.arange(n_items, dtype=jnp.int32)
    wv = jnp.minimum(w, it_end[-1] - 1)
    e_w = jnp.sum((it_end[None, :] <= wv[:, None]).astype(jnp.int32), axis=1)
    oh = (e_w[:, None] == jnp.arange(n_experts, dtype=jnp.int32)[None, :]).astype(jnp.int32)
    pick = lambda v: jnp.sum(oh * v[None, :], axis=1)
    b_w = pick(first_blk) + wv - pick(it_start)
    lo = jnp.maximum(pick(start), b_w * tm) - b_w * tm
    hi = jnp.minimum(pick(end), (b_w + 1) * tm) - b_w * tm
    valid = w < it_end[-1]
    lo = jnp.where(valid, lo, 0)
    hi = jnp.where(valid, hi, 0)
    first = jnp.logical_and(valid, jnp.logical_or(w == 0, b_w != jnp.roll(b_w, 1)))
    new_e = jnp.logical_and(valid, jnp.logical_or(w == 0, e_w != jnp.roll(e_w, 1)))
    i32 = lambda v: v.astype(jnp.int32)
    return i32(b_w), i32(e_w), i32(lo), i32(hi), i32(first), i32(new_e), start


def _expert_kernel(wb, we, wlo, whi, wfirst, wnew, x_ref, wg_ref, bg_ref, wu_ref, bu_ref, wd_ref, bd_ref, o_ref,
                   wg_s, wu_s, wd_s):
    del wb, we
    w = pl.program_id(0)
    lo, hi, first = wlo[w], whi[w], wfirst[w]

    @pl.when(wnew[w] == 1)
    def _():
        wg_s[...] = wg_ref[...].astype(BF16)
        wu_s[...] = wu_ref[...].astype(BF16)
        wd_s[...] = wd_ref[...].astype(BF16)

    @pl.when(hi > lo)
    def _():
        xb = x_ref[...]
        gate = jnp.minimum(jnp.dot(xb, wg_s[...], preferred_element_type=F32) + bg_ref[...], SWIGLU_LIMIT)
        up = jnp.clip(jnp.dot(xb, wu_s[...], preferred_element_type=F32) + bu_ref[...],
                      -SWIGLU_LIMIT, SWIGLU_LIMIT)
        glu = gate * _sigmoid(gate * SWIGLU_ALPHA)
        act = ((up + 1.0) * glu).astype(BF16)
        res = (jnp.dot(act, wd_s[...], preferred_element_type=F32) + bd_ref[...]).astype(BF16)
        row = lax.broadcasted_iota(jnp.int32, res.shape, 0)
        mine = jnp.logical_and(row >= lo, row < hi)

        @pl.when(first == 1)
        def _():
            o_ref[...] = jnp.where(mine, res, jnp.zeros_like(res))

        @pl.when(first == 0)
        def _():
            o_ref[...] = jnp.where(mine, res, o_ref[...])


def _experts(work, xg, layer, wg, bg, wu, bu, wd, bd):
    n_rows, d = xg.shape
    dff = wg.shape[3]
    tm = MOE_TILE
    wspec = lambda r, c: pl.BlockSpec((None, None, r, c), lambda i, wb, we, *_: (layer, we[i], 0, 0))
    xspec = pl.BlockSpec((tm, d), lambda i, wb, *_: (wb[i], 0))
    return pl.pallas_call(
        _expert_kernel,
        grid_spec=pltpu.PrefetchScalarGridSpec(
            num_scalar_prefetch=6,
            grid=(work[0].shape[0],),
            in_specs=[xspec, wspec(d, dff), wspec(1, dff), wspec(d, dff), wspec(1, dff), wspec(dff, d), wspec(1, d)],
            out_specs=xspec,
            scratch_shapes=[pltpu.VMEM((d, dff), BF16), pltpu.VMEM((d, dff), BF16), pltpu.VMEM((dff, d), BF16)]),
        out_shape=jax.ShapeDtypeStruct((n_rows, d), BF16),
        compiler_params=_cparams(("arbitrary",)),
        name="moe_experts",
    )(*work, xg, wg, bg, wu, bu, wd, bd)


def _combine_kernel(x_ref, y_ref, p_ref, g_ref, o_ref):
    rows, d = x_ref.shape
    f = p_ref[:, 0:1] * y_ref[0].astype(F32)
    for k in range(1, TOP_K):
        f = f + p_ref[:, k:k + 1] * y_ref[k].astype(F32)
    o_ref[...] = x_ref[...] + (f.reshape(rows // BS, BS, d) * g_ref[...][None]).reshape(rows, d)


def _combine(x_mid, ypg, probs, g2t, mod_idx):
    n = ypg.shape[1]
    d = x_mid.shape[1]
    tm = ROW_TILE
    row = lambda i: (i, 0)
    return pl.pallas_call(
        _combine_kernel,
        grid=(n // tm,),
        in_specs=[pl.BlockSpec((tm, d), row), pl.BlockSpec((TOP_K, tm, d), lambda i: (0, i, 0)),
                  pl.BlockSpec((tm, ROUTER_PAD), row),
                  pl.BlockSpec((None, BS, d), lambda i: (mod_idx(i), 0, 0))],
        out_specs=pl.BlockSpec((tm, d), row),
        out_shape=jax.ShapeDtypeStruct((n, d), F32),
        compiler_params=_cparams(("arbitrary",)),
        name="moe_combine",
    )(x_mid, ypg, probs, g2t)


def _final_kernel(x_ref, g_ref, o_ref):
    o_ref[...] = _rms(x_ref[...], g_ref[...])


def _final_norm(x, g, n_rows):
    d = x.shape[1]
    tm = ROW_TILE
    return pl.pallas_call(
        _final_kernel,
        grid=(n_rows // tm,),
        in_specs=[pl.BlockSpec((tm, d), lambda i: (i, 0)), pl.BlockSpec((1, d), lambda i: (0, 0))],
        out_specs=pl.BlockSpec((tm, d), lambda i: (i, 0)),
        out_shape=jax.ShapeDtypeStruct((n_rows, d), F32),
        compiler_params=_cparams(("arbitrary",)),
        name="final_rmsnorm",
    )(x, g)


def _to_rows(a):
    b, t, d = a.shape
    return a.reshape(b // BS, BS, t, d).transpose(0, 2, 1, 3).reshape(b * t, d)


def _from_rows(r, b, t):
    d = r.shape[1]
    return r.reshape(b // BS, t, BS, d).transpose(0, 2, 1, 3).reshape(b, t, d)


def kernel(x, c, ctx, c_ctx, ada_w, ada_b, norm1_g, norm2_g, w_in, b_in, conv_w, conv_b, lru_wa, lru_ba, lru_wx, lru_bx, lru_lambda, out_a, pool_w, pool_b, pool_scale, out_b, sg_ln_g, sg_ln_b, sg_w, sg_b, out_c, w_o, b_o, router_w, router_b, w_gate, b_gate, w_up, b_up, w_down, b_down, final_g):
    bsz, seq, d = x.shape
    n_ctx = ctx.shape[1]
    depth = ada_w.shape[0]
    n_experts = router_w.shape[2]
    nbg = bsz // BS
    rl, rc = seq * BS, n_ctx * BS
    geom = (nbg, rl, rc)
    n_lat = nbg * rl
    assert bsz % BS == 0 and seq % GRID_W == 0 and rc % ROW_TILE == 0 and rl % rc == 0
    assert rc % LRU_ROWS == 0 and rc % (CHUNK * BS) == 0 and d // H_A == LANES

    tiles_lat_bg = rl // ROW_TILE
    n_lat_tiles = nbg * tiles_lat_bg
    mod_idx = lambda i: jnp.where(i < n_lat_tiles, i // tiles_lat_bg, nbg)

    xs = jnp.concatenate([_to_rows(x), _to_rows(ctx)], axis=0)

    cc = jnp.concatenate([c, c_ctx[None], jnp.zeros((BS - 1, d), F32)], axis=0)
    mod = _modulation(cc, ada_w, ada_b)
    mod = mod.reshape(depth, bsz + BS, 6, d).transpose(0, 2, 1, 3)
    mod_lat = mod[:, :, :bsz].reshape(depth, 6, nbg, BS, d)
    mod_ctx = jnp.broadcast_to(mod[:, :, bsz:bsz + 1, None, :], (depth, 6, 1, BS, d))
    modt = jnp.concatenate([mod_lat, mod_ctx], axis=2)

    ck = CHUNK * BS
    wk_all = jnp.broadcast_to(sg_w[:, :, :, None, :, None], (depth, G_C, CHUNK, BS, CHUNK, BS)).reshape(depth, G_C, ck, ck)
    same_b = (jnp.arange(ck)[:, None] % BS) == (jnp.arange(ck)[None, :] % BS)
    wk_all = jnp.where(same_b[None, None], wk_all, 0.0).astype(BF16)
    dh = d // H_A
    w_c = d // 2
    for l in range(depth):
        m = modt[l]
        xa, ga, xb, uv, gt = _inproj(xs, m[0:2], norm1_g[l][None], w_in[l].astype(BF16), b_in[l][None], mod_idx)

        hsf = None
        for dr in range(2):
            wgate = (0.5 * jnp.concatenate([lru_wa[l, dr], lru_wx[l, dr]], axis=-1)).astype(BF16)
            bgate = 0.5 * jnp.concatenate([lru_ba[l, dr].reshape(H_A, 1, dh), lru_bx[l, dr].reshape(H_A, 1, dh)], axis=-1)
            args = (xa, conv_w[l], conv_b[l][None], wgate, bgate, lru_lambda[l, dr][None], geom)
            if dr == 0:
                hsf = _lru(*args, reverse=False)
            else:
                la = _lru(*args, reverse=True, hsf=hsf, ga=ga)

        pw = pool_w[l].astype(BF16)
        pbl = _pool(xb, pw, pool_b[l][None], pool_scale[l][None], geom, context=False)
        pbc = _pool(xb, pw, pool_b[l][None], pool_scale[l][None], geom, context=True)

        sbias = jnp.broadcast_to(sg_b[l].T[:, None, :, None], (CHUNK, BS, G_C, w_c // G_C)).reshape(CHUNK * BS, w_c)
        sgo = _sgu(uv, sg_ln_g[l][None], sg_ln_b[l][None], wk_all[l], sbias)

        rw = jnp.pad(router_w[l], ((0, 0), (0, ROUTER_PAD - n_experts)))
        rb = jnp.pad(router_b[l], (0, ROUTER_PAD - n_experts), constant_values=ROUTER_PAD_LOGIT)[None]
        x_mid, h2, logits = _merge(xs, la, pbl, pbc, sgo, gt, m[2:5], out_a[l].astype(BF16), out_b[l].astype(BF16),
                                   out_c[l].astype(BF16), w_o[l].astype(BF16), b_o[l][None], norm2_g[l][None],
                                   rw, rb, mod_idx)

        last = l == depth - 1
        n_tok = n_lat if last else xs.shape[0]
        n_assign = n_tok * TOP_K
        top_e, probs, rank, counts = _route(logits[:n_tok])
        *work, start = _work_list(counts[0, :n_experts], n_assign)
        top_e, rank = top_e[:, :TOP_K], rank[:, :TOP_K]
        e_hot = top_e[:, :, None] == jnp.arange(n_experts, dtype=jnp.int32)[None, None, :]
        pos = rank + jnp.sum(jnp.where(e_hot, start[None, None, :], 0), axis=-1)
        tok_ids = jnp.arange(n_assign, dtype=jnp.int32) // TOP_K
        _, sorted_tok = lax.sort((pos.reshape(-1), tok_ids), num_keys=1)
        xg = h2.at[sorted_tok].get(mode='promise_in_bounds')
        yp = _experts(work, xg, l, w_gate, b_gate[:, :, None], w_up, b_up[:, :, None], w_down, b_down[:, :, None])
        ypg = yp.at[pos.T].get(mode='promise_in_bounds')
        xs = _combine(x_mid, ypg, probs, m[5], mod_idx)

    out = _final_norm(xs, final_g[None], n_lat)
    return _from_rows(out, bsz, seq)
```
